```python
import math
import jax
import jax.numpy as jnp
from jax import lax
import numpy as np

D_MODEL = 2048
BATCH = 32
SEQ = 256
DEPTH = 4
DEC_BATCH = 4
DEC_SEQ = 4096
PAST_LEN = 256

F32 = jnp.float32
GRID_W = 64
N_MIXERS = 4
CHUNK = 64
EPS = 1e-6
GLA_HEADS = 4
GLA_DK = D_MODEL // 2 // GLA_HEADS
GLA_DV = D_MODEL // GLA_HEADS
GLA_GATE_RANK = 16
GLA_GATE_NORM = 16.0
HYENA_ORDER = 2
HYENA_BANDS = 16
HYENA_EMB_DIM = 1 + 2 * HYENA_BANDS
HYENA_FILTER_WIDTH = 64
HYENA_SHORT = 3
HYENA_FAST_DECAY = 0.3
HYENA_SLOW_DECAY = 1.5
HYENA_TARGET = 1e-2
HGRN_DK = 128
HGRN_HEADS = D_MODEL // HGRN_DK
HGRN_DV = D_MODEL // HGRN_HEADS
RWKV_HEAD = 64
RWKV_HEADS = D_MODEL // RWKV_HEAD
RWKV_DECAY_RANK = 96
RWKV_AAA_RANK = 96
RWKV_GATE_RANK = 256
RWKV_GN_EPS = 64e-5
D_FF = 5632
N_EXPERTS = 8
TOP_K = 2
D_FF_EXPERT = 2816

kernel_name = 'hybrid_gla_hyena_hgrn2_rwkv7_diffusion_step'


def _rmsnorm(x, g):
    xf = x.astype(F32)
    xf = xf * lax.rsqrt(jnp.mean(xf * xf, axis=-1, keepdims=True) + EPS)
    return xf.astype(x.dtype) * g


def _adaln(cond, w, b):
    m = jax.nn.silu(cond) @ w + b
    return [t[:, None, :] for t in jnp.split(m, 6, axis=-1)]


def _heads(t, h):
    B, L, _ = t.shape
    return t.reshape(B, L, h, -1).transpose(0, 2, 1, 3)


def _merge(t):
    B, H, L, d = t.shape
    return t.transpose(0, 2, 1, 3).reshape(B, L, H * d)


def _gated_group_norm(o, g, gain):
    of = o * lax.rsqrt(jnp.mean(o * o, axis=-1, keepdims=True) + EPS)
    return _merge(of.astype(g.dtype) * gain) * jax.nn.silu(g)


def _chunk_gla(q, k, v, logf, s0):
    B, H, L, dk = q.shape
    dv = v.shape[-1]
    n = L // CHUNK
    q, k, v, logf = (t.astype(F32).reshape(B, H, n, CHUNK, -1) for t in (q, k, v, logf))
    b = jnp.cumsum(logf, axis=3)
    b_ref = b[:, :, :, CHUNK // 2:CHUNK // 2 + 1]
    b_last = b[:, :, :, -1:]
    scores = jnp.einsum('bhnid,bhnjd->bhnij', q * jnp.exp(b - b_ref), k * jnp.exp(b_ref - b))
    mask = jnp.tril(jnp.ones((CHUNK, CHUNK), dtype=bool))
    o_intra = jnp.einsum('bhnij,bhnje->bhnie', jnp.where(mask, scores, 0.0), v)
    q_inter = q * jnp.exp(b)
    k_upd = k * jnp.exp(b_last - b)
    decay_last = jnp.exp(b_last[:, :, :, 0, :])

    def step(S, xs):
        qi, ku, vi, dl = xs
        o = jnp.einsum('bhid,bhde->bhie', qi, S)
        S = S * dl[..., None] + jnp.einsum('bhjd,bhje->bhde', ku, vi)
        return S, o

    xs = tuple(jnp.moveaxis(t, 2, 0) for t in (q_inter, k_upd, v, decay_last))
    s_last, o_inter = lax.scan(step, s0.astype(F32), xs)
    o = o_intra + jnp.moveaxis(o_inter, 0, 2)
    return o.reshape(B, H, L, dv), s_last


def _bidir_gla(q, ks, v, logfs, s0):
    o_f, s_f = _chunk_gla(q, ks[0], v, logfs[0], s0[:, 0])
    fl = lambda t: jnp.flip(t, axis=2)
    o_b, s_b = _chunk_gla(fl(q), fl(ks[1]), fl(v), fl(logfs[1]), s0[:, 1])
    return o_f + fl(o_b), jnp.stack([s_f, s_b], axis=1)


def gla_mixer(h, s0, w_in, gk_w1, gk_w2, gk_b, norm_g, w_out):
    kd = GLA_HEADS * GLA_DK
    vd = GLA_HEADS * GLA_DV
    q, k, v, g = jnp.split(h @ w_in, [kd, 2 * kd, 2 * kd + vd], axis=-1)
    logfs = [_heads(jax.nn.log_sigmoid(((h @ gk_w1[r]) @ gk_w2[r] + gk_b[r]).astype(F32)) / GLA_GATE_NORM,
                    GLA_HEADS) for r in range(2)]
    q = _heads(q, GLA_HEADS) * GLA_DK ** -0.5
    k = _heads(k, GLA_HEADS)
    o, s = _bidir_gla(q, (k, k), _heads(v, GLA_HEADS), logfs, s0)
    return _gated_group_norm(o, g, norm_g) @ w_out, s


def hgrn2_mixer(h, s0, layer_idx, w_in, lb_raw, norm_g, w_out):
    q, f_fw, f_bw, i, g = jnp.split(h @ w_in, 5, axis=-1)
    lb = jnp.cumsum(jax.nn.softmax(lb_raw.astype(F32), axis=1), axis=1)
    lb = (lb - lb[:, :1])[:, layer_idx]
    ks, logfs = [], []
    for r, fr in enumerate((f_fw, f_bw)):
        f = lb[r] + (1.0 - lb[r]) * jax.nn.sigmoid(fr.astype(F32))
        ks.append(_heads(1.0 - f, HGRN_HEADS))
        logfs.append(_heads(jnp.log(f), HGRN_HEADS))
    q = _heads(jax.nn.silu(q), HGRN_HEADS) * HGRN_DK ** -0.5
    o, s = _bidir_gla(q, ks, _heads(i, HGRN_HEADS), logfs, s0)
    return _gated_group_norm(o, g, norm_g) @ w_out, s


def _short_conv(x, w, b):
    pad = w.shape[0] // 2
    L = x.shape[1]
    xp = jnp.pad(x, ((0, 0), (pad, pad), (0, 0)))
    return sum(xp[:, j:j + L] * w[j] for j in range(w.shape[0])) + b


def _hyena_filters(L, w1, b1, w2, b2, w3, freq):
    pos = jnp.arange(L, dtype=F32)
    t = pos / L
    bands = jnp.arange(1, HYENA_BANDS + 1, dtype=F32)
    ang = (2.0 * math.pi / L) * pos[:, None] * bands[None, :]
    z = jnp.concatenate([t[:, None], jnp.cos(ang), jnp.sin(ang)], axis=-1)
    hid = jnp.sin(freq * (z @ w1 + b1))
    hid = jnp.sin(freq * (hid @ w2 + b2))
    filt = (hid @ w3).astype(F32).reshape(L, HYENA_ORDER, 2, D_MODEL)
    deltas = jnp.abs(jnp.linspace(math.log(HYENA_TARGET) / HYENA_SLOW_DECAY,
                                  math.log(HYENA_TARGET) / HYENA_FAST_DECAY, D_MODEL, dtype=F32))
    filt = filt * jnp.exp(-t[:, None] * deltas)[:, None, None, :]
    two_sided = jnp.concatenate([filt[:, :, 0], jnp.zeros((1, HYENA_ORDER, D_MODEL), F32),
                                 jnp.flip(filt[1:, :, 1], axis=0)], axis=0)
    return two_sided / jnp.sum(jnp.abs(two_sided), axis=0, keepdims=True)


def _fft_conv(u, filt):
    L = u.shape[1]
    U = jnp.fft.rfft(u.astype(F32), n=2 * L, axis=1)
    K = jnp.fft.rfft(filt, axis=0)
    return jnp.fft.irfft(U * K[None], n=2 * L, axis=1)[:, :L]


def hyena_mixer(h, w_in, conv_w, conv_b, f_w1, f_b1, f_w2, f_b2, f_w3, f_freq, f_bias, w_out):
    L = h.shape[1]
    x1, x2, v = jnp.split(_short_conv(h @ w_in, conv_w, conv_b), 3, axis=-1)
    filt = _hyena_filters(L, f_w1, f_b1, f_w2, f_b2, f_w3, f_freq)
    z = v.astype(F32)
    for n, gate in enumerate((x1, x2)):
        z = gate.astype(F32) * (_fft_conv(z, filt[:, n]) + z * f_bias[n])
    return z.astype(h.dtype) @ w_out


def _shift_seq(x):
    half = x.shape[-1] // 2
    prev = jnp.pad(x[:, :-1, :half], ((0, 0), (1, 0), (0, 0)))
    nxt = jnp.pad(x[:, 1:, half:], ((0, 0), (0, 1), (0, 0)))
    return jnp.concatenate([prev, nxt], axis=-1)


def _shift_grid(x):
    B, L, D = x.shape
    rows = L // GRID_W
    g = x.reshape(B, rows, GRID_W, D)
    qd = D // 4
    left = jnp.pad(g[:, :, :-1, :qd], ((0, 0), (0, 0), (1, 0), (0, 0)))
    right = jnp.pad(g[:, :, 1:, qd:2 * qd], ((0, 0), (0, 0), (0, 1), (0, 0)))
    up = jnp.pad(g[:, :-1, :, 2 * qd:3 * qd], ((0, 0), (1, 0), (0, 0), (0, 0)))
    down = jnp.pad(g[:, 1:, :, 3 * qd:], ((0, 0), (0, 1), (0, 0), (0, 0)))
    return jnp.concatenate([left, right, up, down], axis=-1).reshape(B, L, D)


def _rwkv7_scan(r, logw, k, v, kk, a, s0):
    def step(S, xs):
        rt, wt, kt, vt, kkt, at = xs
        sa = jnp.einsum('bhvk,bhk->bhv', S, kkt)
        S = (S * jnp.exp(wt)[:, :, None, :]
             - jnp.einsum('bhv,bhk->bhvk', sa, kkt * at)
             + jnp.einsum('bhv,bhk->bhvk', vt, kt))
        return S, jnp.einsum('bhvk,bhk->bhv', S, rt)

    xs = tuple(jnp.moveaxis(t, 1, 0) for t in (r, logw, k, v, kk, a))
    s_last, y = lax.scan(step, s0.astype(F32), xs)
    return jnp.moveaxis(y, 0, 1), s_last


def rwkv7_mixer(h, s0, shift_fn, mu, w_r, w_k, w_v, w_o, w0, w1, w2, a0, a1, a2, g1, g2,
                k_k, k_a, r_k, ln_w, ln_b):
    B, L, _ = h.shape
    hd = lambda t: t.astype(F32).reshape(B, L, RWKV_HEADS, RWKV_HEAD)
    xx = shift_fn(h) - h
    xr, xw, xk, xv, xa, xg = (h + xx * mu[j] for j in range(6))
    r = xr @ w_r
    k = xk @ w_k
    v = xv @ w_v
    gate = jax.nn.sigmoid(xg @ g1) @ g2
    kk = hd(k * k_k)
    kk = kk * lax.rsqrt(jnp.sum(kk * kk, axis=-1, keepdims=True) + 1e-12)
    rh, vh = hd(r), hd(v)
    rk = r_k.reshape(RWKV_HEADS, RWKV_HEAD)
    ys, bonus, states = [], [], []
    for d in range(2):
        fl = (lambda t: t) if d == 0 else (lambda t: jnp.flip(t, axis=1))
        logw = -jnp.exp(-jax.nn.softplus(-(w0[d] + jnp.tanh(xw @ w1[d]) @ w2[d])) - 0.5)
        a = jax.nn.sigmoid(a0[d] + (xa @ a1[d]) @ a2[d])
        kd = hd(k * (1.0 + (a - 1.0) * k_a))
        yd, sd = _rwkv7_scan(fl(rh), fl(hd(logw)), fl(kd), fl(vh), fl(kk), fl(hd(a)), s0[:, d])
        ys.append(fl(yd))
        bonus.append(jnp.sum(rh * kd * rk, axis=-1, keepdims=True) * vh)
        states.append(sd)
    y = ys[0] + ys[1]
    mean = jnp.mean(y, axis=-1, keepdims=True)
    var = jnp.mean(jnp.square(y - mean), axis=-1, keepdims=True)
    y = ((y - mean) * lax.rsqrt(var + RWKV_GN_EPS)).reshape(B, L, D_MODEL) * ln_w + ln_b
    y = (y + (bonus[0] + bonus[1]).reshape(B, L, D_MODEL)).astype(h.dtype) * gate
    return y @ w_o, jnp.stack(states, axis=1)


def _swiglu(x, w1, w3, w2):
    return (jax.nn.silu(x @ w1) * (x @ w3)) @ w2


def _moe(x, router, w1, w3, w2):
    logits = (x @ router).astype(F32)
    top_val, top_idx = lax.top_k(logits, TOP_K)
    gate = jnp.sum(jax.nn.one_hot(top_idx, N_EXPERTS, dtype=F32)
                   * jax.nn.softmax(top_val, axis=-1)[..., None], axis=-2)
    out = jnp.zeros(x.shape, F32)
    for e in range(N_EXPERTS):
        out = out + gate[..., e:e + 1] * _swiglu(x, w1[e], w3[e], w2[e])
    return out.astype(x.dtype)


def setup_inputs(seed: int = 0) -> dict:
    key = jax.random.key(seed)
    ks = iter(jax.random.split(key, 64))
    D = D_MODEL
    n_dense = (DEPTH + 1) // 2
    n_moe = DEPTH // 2

    def nrm(shape, scale):
        return jax.random.normal(next(ks), shape, F32) * scale

    kd = GLA_HEADS * GLA_DK
    vd = GLA_HEADS * GLA_DV
    return {
        'x_prompt': nrm((BATCH, SEQ, D), 1.0),
        'x_sample': nrm((DEC_BATCH, DEC_SEQ, D), 1.0),
        'state_l0_gla': nrm((DEC_BATCH, 2, GLA_HEADS, GLA_DK, GLA_DV), 0.5),
        'state_l2_hgrn2': nrm((DEC_BATCH, 2, HGRN_HEADS, HGRN_DK, HGRN_DV), 0.5),
        'state_l3_rwkv7': nrm((DEC_BATCH, 2, RWKV_HEADS, RWKV_HEAD, RWKV_HEAD), 0.5),
        'c': nrm((DEC_BATCH, D), 1.0),
        'c_ctx': nrm((D,), 1.0),
        'norm_mix_g': 1.0 + nrm((DEPTH, D), 0.02),
        'norm_ffn_g': 1.0 + nrm((DEPTH, D), 0.02),
        'mod_w': nrm((DEPTH, D, 6 * D), 0.5 * D ** -0.5),
        'mod_b': nrm((DEPTH, 6 * D), 0.01),
        'gla_w_in': nrm((D, 2 * kd + 2 * vd), D ** -0.5),
        'gla_gk_w1': nrm((2, D, GLA_GATE_RANK), D ** -0.5),
        'gla_gk_w2': nrm((2, GLA_GATE_RANK, kd), GLA_GATE_RANK ** -0.5),
        'gla_gk_b': nrm((2, kd), 0.5),
        'gla_norm_g': 1.0 + nrm((GLA_DV,), 0.02),
        'gla_w_out': nrm((vd, D), vd ** -0.5),
        'hy_w_in': nrm((D, 3 * D), D ** -0.5),
        'hy_conv_w': nrm((HYENA_SHORT, 3 * D), 0.5),
        'hy_conv_b': nrm((3 * D,), 0.01),
        'hy_f_w1': nrm((HYENA_EMB_DIM, HYENA_FILTER_WIDTH), HYENA_EMB_DIM ** -0.5),
        'hy_f_b1': nrm((HYENA_FILTER_WIDTH,), 0.1),
        'hy_f_w2': nrm((HYENA_FILTER_WIDTH, HYENA_FILTER_WIDTH), HYENA_FILTER_WIDTH ** -0.5),
        'hy_f_b2': nrm((HYENA_FILTER_WIDTH,), 0.1),
        'hy_f_w3': nrm((HYENA_FILTER_WIDTH, HYENA_ORDER * 2 * D), HYENA_FILTER_WIDTH ** -0.5),
        'hy_f_freq': 1.0 + nrm((HYENA_FILTER_WIDTH,), 0.1),
        'hy_f_bias': nrm((HYENA_ORDER, D), 0.5),
        'hy_w_out': nrm((D, D), D ** -0.5),
        'hg_w_in': nrm((D, 5 * D), D ** -0.5),
        'hg_lb': 1.0 + nrm((2, DEPTH, D), 0.1),
        'hg_norm_g': 1.0 + nrm((HGRN_DV,), 0.02),
        'hg_w_out': nrm((D, D), D ** -0.5),
        'rw_mu': jax.random.uniform(next(ks), (6, D), F32),
        'rw_w_r': nrm((D, D), D ** -0.5),
        'rw_w_k': nrm((D, D), D ** -0.5),
        'rw_w_v': nrm((D, D), D ** -0.5),
        'rw_w_o': nrm((D, D), D ** -0.5),
        'rw_w0': nrm((2, D), 1.0) - 1.0,
        'rw_w1': nrm((2, D, RWKV_DECAY_RANK), D ** -0.5),
        'rw_w2': nrm((2, RWKV_DECAY_RANK, D), RWKV_DECAY_RANK ** -0.5),
        'rw_a0': nrm((2, D), 0.5),
        'rw_a1': nrm((2, D, RWKV_AAA_RANK), D ** -0.5),
        'rw_a2': nrm((2, RWKV_AAA_RANK, D), RWKV_AAA_RANK ** -0.5),
        'rw_g1': nrm((D, RWKV_GATE_RANK), D ** -0.5),
        'rw_g2': nrm((RWKV_GATE_RANK, D), RWKV_GATE_RANK ** -0.5),
        'rw_k_k': 0.85 + nrm((D,), 0.05),
        'rw_k_a': 1.0 + nrm((D,), 0.05),
        'rw_r_k': nrm((D,), 0.1),
        'rw_ln_w': 1.0 + nrm((D,), 0.02),
        'rw_ln_b': nrm((D,), 0.01),
        'ffn_w1': nrm((n_dense, D, D_FF), D ** -0.5),
        'ffn_w3': nrm((n_dense, D, D_FF), D ** -0.5),
        'ffn_w2': nrm((n_dense, D_FF, D), D_FF ** -0.5),
        'moe_router': nrm((n_moe, D, N_EXPERTS), D ** -0.5),
        'moe_w1': nrm((n_moe, N_EXPERTS, D, D_FF_EXPERT), D ** -0.5),
        'moe_w3': nrm((n_moe, N_EXPERTS, D, D_FF_EXPERT), D ** -0.5),
        'moe_w2': nrm((n_moe, N_EXPERTS, D_FF_EXPERT, D), D_FF_EXPERT ** -0.5),
        'final_norm_g': 1.0 + nrm((D,), 0.02),
    }


def reference(x_prompt, x_sample, state_l0_gla, state_l2_hgrn2, state_l3_rwkv7, c, c_ctx,
              norm_mix_g, norm_ffn_g, mod_w, mod_b,
              gla_w_in, gla_gk_w1, gla_gk_w2, gla_gk_b, gla_norm_g, gla_w_out,
              hy_w_in, hy_conv_w, hy_conv_b, hy_f_w1, hy_f_b1, hy_f_w2, hy_f_b2, hy_f_w3, hy_f_freq,
              hy_f_bias, hy_w_out,
              hg_w_in, hg_lb, hg_norm_g, hg_w_out,
              rw_mu, rw_w_r, rw_w_k, rw_w_v, rw_w_o, rw_w0, rw_w1, rw_w2, rw_a0, rw_a1, rw_a2,
              rw_g1, rw_g2, rw_k_k, rw_k_a, rw_r_k, rw_ln_w, rw_ln_b,
              ffn_w1, ffn_w3, ffn_w2, moe_router, moe_w1, moe_w3, moe_w2, final_norm_g):
    xp, xs = x_prompt, x_sample
    bp = x_prompt.shape[0]
    gla_p = (gla_w_in, gla_gk_w1, gla_gk_w2, gla_gk_b, gla_norm_g, gla_w_out)
    hy_p = (hy_w_in, hy_conv_w, hy_conv_b, hy_f_w1, hy_f_b1, hy_f_w2, hy_f_b2, hy_f_w3, hy_f_freq,
            hy_f_bias, hy_w_out)
    hg_p = (hg_w_in, hg_lb, hg_norm_g, hg_w_out)
    rw_p = (rw_mu, rw_w_r, rw_w_k, rw_w_v, rw_w_o, rw_w0, rw_w1, rw_w2, rw_a0, rw_a1, rw_a2,
            rw_g1, rw_g2, rw_k_k, rw_k_a, rw_r_k, rw_ln_w, rw_ln_b)
    for l in range(DEPTH):
        kind = l % N_MIXERS
        mp = _adaln(c_ctx[None, :], mod_w[l], mod_b[l])
        ms = _adaln(c, mod_w[l], mod_b[l])
        hp = _rmsnorm(xp, norm_mix_g[l]) * (1.0 + mp[1]) + mp[0]
        hs = _rmsnorm(xs, norm_mix_g[l]) * (1.0 + ms[1]) + ms[0]
        if kind == 0:
            s_zero = jnp.zeros((bp, 2, GLA_HEADS, GLA_DK, GLA_DV), F32)
            op, new_state_l0_gla = gla_mixer(hp, s_zero, *gla_p)
            os_, _ = gla_mixer(hs, state_l0_gla, *gla_p)
        elif kind == 1:
            op = hyena_mixer(hp, *hy_p)
            os_ = hyena_mixer(hs, *hy_p)
        elif kind == 2:
            s_zero = jnp.zeros((bp, 2, HGRN_HEADS, HGRN_DK, HGRN_DV), F32)
            op, new_state_l2_hgrn2 = hgrn2_mixer(hp, s_zero, l, *hg_p)
            os_, _ = hgrn2_mixer(hs, state_l2_hgrn2, l, *hg_p)
        else:
            s_zero = jnp.zeros((bp, 2, RWKV_HEADS, RWKV_HEAD, RWKV_HEAD), F32)
            op, new_state_l3_rwkv7 = rwkv7_mixer(hp, s_zero, _shift_seq, *rw_p)
            os_, _ = rwkv7_mixer(hs, state_l3_rwkv7, _shift_grid, *rw_p)
        xp = xp + mp[2] * op
        xs = xs + ms[2] * os_
        hp = _rmsnorm(xp, norm_ffn_g[l]) * (1.0 + mp[4]) + mp[3]
        hs = _rmsnorm(xs, norm_ffn_g[l]) * (1.0 + ms[4]) + ms[3]
        j = l // 2
        if l % 2 == 0:
            fp = _swiglu(hp, ffn_w1[j], ffn_w3[j], ffn_w2[j])
            fs = _swiglu(hs, ffn_w1[j], ffn_w3[j], ffn_w2[j])
        else:
            fp = _moe(hp, moe_router[j], moe_w1[j], moe_w3[j], moe_w2[j])
            fs = _moe(hs, moe_router[j], moe_w1[j], moe_w3[j], moe_w2[j])
        xp = xp + mp[5] * fp
        xs = xs + ms[5] * fs
    y_prompt = _rmsnorm(xp, final_norm_g)
    y_sample = _rmsnorm(xs, final_norm_g)
    return (y_prompt, y_sample, new_state_l0_gla, new_state_l2_hgrn2, new_state_l3_rwkv7)
```

```python
import collections
import functools
import math

import jax
import jax.numpy as jnp
from jax import lax
from jax.experimental import pallas as pl
from jax.experimental.pallas import tpu as pltpu

F32 = jnp.float32
BF16 = jnp.bfloat16
HIGHEST = lax.Precision.HIGHEST

D = 2048
DEPTH = 4
CHUNK = 64
EPS = 1e-6
GRID_W = 64
GLA_HEADS, GLA_DK, GLA_DV = 4, 256, 512
GLA_GATE_RANK, GLA_GATE_NORM = 16, 16.0
HY_ORDER, HY_BANDS, HY_WIDTH = 2, 16, 64
HY_FAST, HY_SLOW, HY_TARGET = 0.3, 1.5, 1e-2
HG_HEADS, HG_DK, HG_DV = 16, 128, 128
RW_HEADS, RW_HEAD = 32, 64
RW_RANK = 96
RW_GN_EPS = 64e-5
N_EXPERTS = 8
LANES = 128
VMEM_LIMIT_MB = 56
LINEAR_VMEM_BUDGET = 44 * 1024 * 1024
HY_BLOCK_MAX = 1024

Lay = collections.namedtuple("Lay", "B0 L0 B1 L1")


def _R0(lay):
    return lay.B0 * lay.L0


def _M(lay):
    return lay.B0 * lay.L0 + lay.B1 * lay.L1


def _group_of_row(lay, r):
    r0 = _R0(lay)
    return jnp.where(r < r0, 0, 1 + (r - r0) // lay.L1)


def _cparams(n_axes, vmem_mb=VMEM_LIMIT_MB):
    return pltpu.CompilerParams(dimension_semantics=("arbitrary",) * n_axes,
                                vmem_limit_bytes=vmem_mb * 1024 * 1024)


def _row_tile(lay, cap):
    t = cap
    while (_R0(lay) % t) or (lay.L1 % t) or (lay.L0 % t and t % lay.L0):
        t //= 2
    return t


def _sigmoid(x):
    return 1.0 / (1.0 + jnp.exp(-x))


def _softplus(x):
    return jnp.maximum(x, 0.0) + jnp.log(1.0 + jnp.exp(-jnp.abs(x)))


def _linear_kernel(*refs, n_w, cache_w, has_bias, has_res, has_mod, has_row, row_col, act,
                   pre_silu, swiglu, hp):
    it = iter(refs)
    x_ref = next(it)
    w_refs = [next(it) for _ in range(n_w)]
    b_ref = next(it) if has_bias else None
    res_ref = next(it) if has_res else None
    mod_ref = next(it) if has_mod else None
    row_ref = next(it) if has_row else None
    o_ref = next(it)
    scr = [next(it) for _ in range(n_w)] if cache_w else None

    x = x_ref[...]
    if pre_silu:
        x = x.astype(F32)
        x = x * _sigmoid(x)
    if hp:
        accs = [jnp.dot(x.astype(F32), w[...].astype(F32), precision=HIGHEST,
                        preferred_element_type=F32) for w in w_refs]
    else:
        if cache_w:
            @pl.when(pl.program_id(1) == 0)
            def _():
                for s, w in zip(scr, w_refs):
                    s[...] = w[...].astype(BF16)
            wv = [s[...] for s in scr]
        else:
            wv = [w[...].astype(BF16) for w in w_refs]
        xb = x.astype(BF16)
        accs = [jnp.dot(xb, w, preferred_element_type=F32) for w in wv]
    if swiglu:
        a = accs[0]
        acc = a * _sigmoid(a) * accs[1]
    else:
        acc = accs[0]
    if has_bias:
        acc = acc + b_ref[...]
    if act == "tanh":
        acc = jnp.tanh(acc)
    elif act == "sigmoid":
        acc = _sigmoid(acc)
    if has_row:
        acc = acc * row_ref[:, row_col:row_col + 1]
    if has_mod:
        acc = acc * mod_ref[...]
    if has_res:
        acc = res_ref[...] + acc
    o_ref[...] = acc.astype(o_ref.dtype)


def _linear(x, ws, *, tm, tn, out_dtype=F32, w_idx=None, x_idx=None, bias=None, act=None,
            pre_silu=False, swiglu=False, res=None, modgate=None, lay=None, rowgate=None,
            rowgate_col=0, hp=False):
    n_w = len(ws)
    M, K = x.shape[-2:]
    N = ws[0].shape[-1]
    tm = min(tm, M)
    tn = min(tn, N)

    def vmem_bytes(tm_):
        cache = (not hp) and M // tm_ > 1 and ws[0].dtype != BF16
        b = 2 * tm_ * K * x.dtype.itemsize
        b += n_w * K * tn * (2 * ws[0].dtype.itemsize + (2 if cache else 0))
        b += 2 * tm_ * tn * jnp.dtype(out_dtype).itemsize + 3 * tm_ * tn * 4
        b += 2 * tm_ * tn * 4 if res is not None else 0
        b += 2 * tm_ * LANES * 4 if rowgate is not None else 0
        return b

    while vmem_bytes(tm) > LINEAR_VMEM_BUDGET and tm > 256:
        tm //= 2
    assert M % tm == 0 and N % tn == 0, (M, tm, N, tn)
    n_i = M // tm
    cache_w = (not hp) and n_i > 1 and ws[0].dtype != BF16

    if x_idx is None:
        x_spec = pl.BlockSpec((tm, K), lambda j, i: (i, 0))
    else:
        x_spec = pl.BlockSpec((None, tm, K), lambda j, i: (x_idx, i, 0))
    if w_idx is None:
        w_spec = pl.BlockSpec((K, tn), lambda j, i: (0, j))
    else:
        w_spec = pl.BlockSpec((None, K, tn), lambda j, i: (w_idx, 0, j))
    in_specs = [x_spec] + [w_spec] * n_w
    args = [x] + list(ws)
    if bias is not None:
        in_specs.append(pl.BlockSpec((1, tn), lambda j, i: (0, j)))
        args.append(bias.reshape(1, N).astype(F32))
    if res is not None:
        in_specs.append(pl.BlockSpec((tm, tn), lambda j, i: (i, j)))
        args.append(res)
    if modgate is not None:
        in_specs.append(pl.BlockSpec((None, 1, tn), lambda j, i: (_group_of_row(lay, i * tm), 0, j)))
        args.append(modgate)
    if rowgate is not None:
        in_specs.append(pl.BlockSpec((tm, LANES), lambda j, i: (i, 0)))
        args.append(rowgate)
    kern = functools.partial(
        _linear_kernel, n_w=n_w, cache_w=cache_w, has_bias=bias is not None, has_res=res is not None,
        has_mod=modgate is not None, has_row=rowgate is not None, row_col=rowgate_col, act=act,
        pre_silu=pre_silu, swiglu=swiglu, hp=hp)
    scratch = [pltpu.VMEM((K, tn), BF16) for _ in range(n_w)] if cache_w else []
    return pl.pallas_call(
        kern,
        grid=(N // tn, n_i),
        in_specs=in_specs,
        out_specs=pl.BlockSpec((tm, tn), lambda j, i: (i, j)),
        out_shape=jax.ShapeDtypeStruct((M, N), out_dtype),
        scratch_shapes=scratch,
        compiler_params=_cparams(2),
    )(*args)


def _norm_kernel(*refs, has_mod):
    if has_mod:
        x_ref, g_ref, sh_ref, sc_ref, o_ref = refs
    else:
        x_ref, g_ref, o_ref = refs
    x = x_ref[...]
    y = x * lax.rsqrt(jnp.mean(x * x, axis=-1, keepdims=True) + EPS) * g_ref[...]
    if has_mod:
        y = y * (1.0 + sc_ref[...]) + sh_ref[...]
    o_ref[...] = y.astype(o_ref.dtype)


def _norm(x, g, *, lay, tm, out_dtype, shift=None, scale=None, row0=0, nrows=None):
    nrows = x.shape[0] if nrows is None else nrows
    assert row0 % tm == 0 and nrows % tm == 0
    off = row0 // tm
    has_mod = shift is not None
    in_specs = [pl.BlockSpec((tm, D), lambda i: (i + off, 0)),
                pl.BlockSpec((1, D), lambda i: (0, 0))]
    args = [x, g.reshape(1, D)]
    if has_mod:
        mspec = pl.BlockSpec((None, 1, D), lambda i: (_group_of_row(lay, (i + off) * tm), 0, 0))
        in_specs += [mspec, mspec]
        args += [shift, scale]
    return pl.pallas_call(
        functools.partial(_norm_kernel, has_mod=has_mod),
        grid=(nrows // tm,),
        in_specs=in_specs,
        out_specs=pl.BlockSpec((tm, D), lambda i: (i, 0)),
        out_shape=jax.ShapeDtypeStruct((nrows, D), out_dtype),
        compiler_params=_cparams(1),
    )(*args)


def _scan_kernel(*refs, mode, reverse, T, Hb, dk, dv, zero_init, emit_state, nT):
    it = iter(refs)
    q_ref = next(it)
    k_ref = next(it) if mode == "gla" else None
    f_ref = next(it)
    v_ref = next(it)
    lb_ref = next(it) if mode == "hgrn" else None
    s0_ref = None if zero_init else next(it)
    o_ref = next(it)
    so_ref = next(it) if emit_state else None
    s_scr = next(it)
    C = CHUNK
    t = pl.program_id(2)

    @pl.when(t == 0)
    def _():
        for hh in range(Hb):
            if zero_init:
                s_scr[hh] = jnp.zeros((dv, dk), F32)
            else:
                s_scr[hh] = s0_ref[hh].T

    row = lax.broadcasted_iota(jnp.int32, (C, C), 0)
    col = lax.broadcasted_iota(jnp.int32, (C, C), 1)
    incl = (row <= col) if reverse else (row >= col)
    tri = incl.astype(F32)
    ref_i = (C - 1 - C // 2) if reverse else C // 2
    last_i = 0 if reverse else C - 1
    nc = T // C
    nt_dims = (((1,), (1,)), ((), ()))
    tn_dims = (((0,), (0,)), ((), ()))

    def body(i, carry):
        cc = (nc - 1 - i) if reverse else i
        rows = pl.ds(pl.multiple_of(cc * C, C), C)
        for hh in range(Hb):
            ks = slice(hh * dk, (hh + 1) * dk)
            vs = slice(hh * dv, (hh + 1) * dv)
            q = q_ref[rows, ks]
            fx = f_ref[rows, ks]
            v = v_ref[rows, vs]
            if mode == "gla":
                k = k_ref[rows, ks]
                logf = (jnp.minimum(fx, 0.0) - jnp.log(1.0 + jnp.exp(-jnp.abs(fx)))) * (1.0 / GLA_GATE_NORM)
                q = q * dk ** -0.5
            else:
                lb = lb_ref[:, ks]
                f = lb + (1.0 - lb) * _sigmoid(fx)
                k = 1.0 - f
                logf = jnp.log(f)
                q = q * _sigmoid(q) * dk ** -0.5
            b = jnp.dot(tri, logf, precision=HIGHEST, preferred_element_type=F32)
            b_mid = b[ref_i:ref_i + 1, :]
            b_last = b[last_i:last_i + 1, :]
            qe = (q * jnp.exp(b - b_mid)).astype(BF16)
            ke = (k * jnp.exp(b_mid - b)).astype(BF16)
            sc = lax.dot_general(qe, ke, nt_dims, preferred_element_type=F32)
            sc = jnp.where(incl, sc, 0.0).astype(BF16)
            vb = v.astype(BF16)
            st = s_scr[hh]
            qi = (q * jnp.exp(b)).astype(BF16)
            o = jnp.dot(sc, vb, preferred_element_type=F32)
            o = o + lax.dot_general(qi, st.astype(BF16), nt_dims, preferred_element_type=F32)
            o_ref[rows, vs] = o
            ku = (k * jnp.exp(b_last - b)).astype(BF16)
            upd = lax.dot_general(vb, ku, tn_dims, preferred_element_type=F32)
            s_scr[hh] = st * jnp.exp(b_last) + upd
        return carry

    lax.fori_loop(0, nc, body, 0)

    if emit_state:
        @pl.when(t == nT - 1)
        def _():
            for hh in range(Hb):
                so_ref[hh] = s_scr[hh].T


def _scan(mode, lay, seg, direction, proj, fsrc, lb, s0, o_prev):
    M = _M(lay)
    if mode == "gla":
        H, dk, dv, Hb = GLA_HEADS, GLA_DK, GLA_DV, 1
        q_c, k_c, f_c, v_c = 0, GLA_HEADS, direction * GLA_HEADS, 2 * GLA_HEADS * GLA_DK // GLA_DV
    else:
        H, dk, dv, Hb = HG_HEADS, HG_DK, HG_DV, 4
        nb = HG_HEADS // Hb
        q_c, k_c, f_c, v_c = 0, None, (1 + direction) * nb, 3 * nb
    if seg == 0:
        B, L, boff = lay.B0, lay.L0, 0
    else:
        B, L, boff = lay.B1, lay.L1, _R0(lay) // lay.L1
        assert _R0(lay) % lay.L1 == 0
    reverse = direction == 1
    T = min(L, 512)
    nT = L // T
    zero_init = s0 is None
    emit_state = seg == 0
    W = proj.shape[1]
    pv = proj.reshape(M // L, L, W)
    fv = fsrc.reshape(M // L, L, fsrc.shape[1])

    def tt(t):
        return (nT - 1 - t) if reverse else t

    def cspec(width, c0):
        return pl.BlockSpec((None, T, width), lambda b, h, t: (b + boff, tt(t), c0 + h))

    in_specs = [cspec(Hb * dk, q_c)]
    args = [pv]
    if mode == "gla":
        in_specs.append(cspec(Hb * dk, k_c))
        args.append(pv)
    in_specs.append(cspec(Hb * dk, f_c))
    args.append(fv)
    in_specs.append(cspec(Hb * dv, v_c))
    args.append(pv)
    if mode == "hgrn":
        in_specs.append(pl.BlockSpec((None, 1, Hb * dk), lambda b, h, t: (direction, 0, h)))
        args.append(lb)
    if not zero_init:
        in_specs.append(pl.BlockSpec((None, None, Hb, dk, dv), lambda b, h, t: (b, direction, h, 0, 0)))
        args.append(s0)
    aliases = {}
    if o_prev is not None:
        in_specs.append(pl.BlockSpec(memory_space=pl.ANY))
        args.append(o_prev.reshape(M // L, L, H * dv))
        aliases = {len(args) - 1: 0}
    out_specs = [pl.BlockSpec((None, T, Hb * dv), lambda b, h, t: (b + boff, tt(t), h))]
    out_shape = [jax.ShapeDtypeStruct((M // L, L, H * dv), F32)]
    if emit_state:
        out_specs.append(pl.BlockSpec((None, Hb, dk, dv), lambda b, h, t: (b, h, 0, 0)))
        out_shape.append(jax.ShapeDtypeStruct((B, H, dk, dv), F32))

    def kern(*refs):
        refs = list(refs)
        if o_prev is not None:
            n_in = len(args)
            refs.pop(n_in - 1)
        _scan_kernel(*refs, mode=mode, reverse=reverse, T=T, Hb=Hb, dk=dk, dv=dv,
                     zero_init=zero_init, emit_state=emit_state, nT=nT)

    outs = pl.pallas_call(
        kern,
        grid=(B, H // Hb, nT),
        in_specs=in_specs,
        out_specs=out_specs,
        out_shape=out_shape,
        scratch_shapes=[pltpu.VMEM((Hb, dv, dk), F32)],
        input_output_aliases=aliases,
        compiler_params=_cparams(3),
    )(*args)
    o = outs[0].reshape(M, H * dv)
    return o, (outs[1] if emit_state else None)


def _gated_norm_kernel(of_ref, ob_ref, g_ref, gain_ref, o_ref, *, H, dv):
    gain = gain_ref[...]
    for h in range(H):
        cs = slice(h * dv, (h + 1) * dv)
        o = of_ref[:, cs] + ob_ref[:, cs]
        o = o * lax.rsqrt(jnp.mean(o * o, axis=-1, keepdims=True) + EPS) * gain
        g = g_ref[:, cs]
        o_ref[:, cs] = (o * (g * _sigmoid(g))).astype(o_ref.dtype)


def _gated_norm(o_f, o_b, proj, g_col, gain, H, dv, tm):
    M = o_f.shape[0]
    return pl.pallas_call(
        functools.partial(_gated_norm_kernel, H=H, dv=dv),
        grid=(M // tm,),
        in_specs=[pl.BlockSpec((tm, D), lambda i: (i, 0)),
                  pl.BlockSpec((tm, D), lambda i: (i, 0)),
                  pl.BlockSpec((tm, D), lambda i: (i, g_col)),
                  pl.BlockSpec((1, dv), lambda i: (0, 0))],
        out_specs=pl.BlockSpec((tm, D), lambda i: (i, 0)),
        out_shape=jax.ShapeDtypeStruct((M, D), BF16),
        compiler_params=_cparams(1),
    )(o_f, o_b, proj, gain.reshape(1, dv))


def _bidir_scan(mode, lay, proj, fsrc, lb, s0_sample):
    outs, states = [], []
    for d in range(2):
        o, st = _scan(mode, lay, 0, d, proj, fsrc, lb, None, None)
        o, _ = _scan(mode, lay, 1, d, proj, fsrc, lb, s0_sample, o)
        outs.append(o)
        states.append(st)
    return outs[0], outs[1], jnp.stack(states, axis=1)


def _gla_mixer(lay, h, x, mgate, s0, w_in, gk_w1, gk_w2, gk_b, norm_g, w_out, tm):
    kd = GLA_HEADS * GLA_DK
    proj = _linear(h, [w_in], tm=tm, tn=1024)
    w1cat = jnp.zeros((D, LANES), F32).at[:, :GLA_GATE_RANK].set(gk_w1[0])
    w1cat = w1cat.at[:, GLA_GATE_RANK:2 * GLA_GATE_RANK].set(gk_w1[1])
    w2cat = jnp.zeros((LANES, 2 * kd), F32).at[:GLA_GATE_RANK, :kd].set(gk_w2[0])
    w2cat = w2cat.at[GLA_GATE_RANK:2 * GLA_GATE_RANK, kd:].set(gk_w2[1])
    low = _linear(h, [w1cat], tm=tm, tn=LANES)
    gk = _linear(low, [w2cat], tm=tm, tn=1024, bias=gk_b.reshape(2 * kd))
    o_f, o_b, new_state = _bidir_scan("gla", lay, proj, gk, None, s0)
    y = _gated_norm(o_f, o_b, proj, 2, norm_g, GLA_HEADS, GLA_DV, min(tm, 256))
    x = _linear(y, [w_out], tm=tm, tn=1024, res=x, modgate=mgate, lay=lay)
    return x, new_state


def _hgrn2_mixer(lay, h, x, mgate, s0, layer_idx, w_in, lb_raw, norm_g, w_out, tm):
    proj = _linear(h, [w_in], tm=tm, tn=1024)
    lb = jnp.cumsum(jax.nn.softmax(lb_raw.astype(F32), axis=1), axis=1)
    lb = (lb - lb[:, :1])[:, layer_idx].reshape(2, 1, D)
    o_f, o_b, new_state = _bidir_scan("hgrn", lay, proj, proj, lb, s0)
    y = _gated_norm(o_f, o_b, proj, 4, norm_g, HG_HEADS, HG_DV, min(tm, 256))
    x = _linear(y, [w_out], tm=tm, tn=1024, res=x, modgate=mgate, lay=lay)
    return x, new_state


def _seq_pos(lay, r):
    r0 = _R0(lay)
    is_p = r < r0
    pos = jnp.where(is_p, r % lay.L0, (r - r0) % lay.L1)
    return is_p, pos, jnp.where(is_p, lay.L0, lay.L1)


def _hy_conv_kernel(x_ref, xp_ref, xn_ref, w_ref, b_ref, o_ref, *, lay, tm):
    _, pos0, seq_len = _seq_pos(lay, pl.program_id(0) * tm)
    x = x_ref[...]
    rows = lax.broadcasted_iota(jnp.int32, (tm, 1), 0)
    prev_row = jnp.where(pos0 == 0, 0.0, xp_ref[7:8, :])
    next_row = jnp.where(pos0 + tm == seq_len, 0.0, xn_ref[0:1, :])
    x_m1 = jnp.where(rows == 0, prev_row, pltpu.roll(x, 1, 0))
    x_p1 = jnp.where(rows == tm - 1, next_row, pltpu.roll(x, tm - 1, 0))
    w = w_ref[...]
    o_ref[...] = x_m1 * w[0:1] + x * w[1:2] + x_p1 * w[2:3] + b_ref[...]


def _hy_conv(lay, x, w, b, tm, tc):
    M, W = x.shape
    sub = 8
    return pl.pallas_call(
        functools.partial(_hy_conv_kernel, lay=lay, tm=tm),
        grid=(M // tm, W // tc),
        in_specs=[pl.BlockSpec((tm, tc), lambda i, j: (i, j)),
                  pl.BlockSpec((sub, tc), lambda i, j: (jnp.maximum(i * (tm // sub) - 1, 0), j)),
                  pl.BlockSpec((sub, tc), lambda i, j: (jnp.minimum((i + 1) * (tm // sub), M // sub - 1), j)),
                  pl.BlockSpec((3, tc), lambda i, j: (0, j)),
                  pl.BlockSpec((1, tc), lambda i, j: (0, j))],
        out_specs=pl.BlockSpec((tm, tc), lambda i, j: (i, j)),
        out_shape=jax.ShapeDtypeStruct((M, W), F32),
        compiler_params=_cparams(2),
    )(x, x, x, w, b.reshape(1, W))


def _hy_geom(L):
    P = min(L, HY_BLOCK_MAX)
    return P, L // P, P + LANES


def _dft_mats(P):
    N, Pp = 2 * P, P + LANES
    k = jnp.arange(Pp, dtype=jnp.int32)[:, None]
    m = jnp.arange(N, dtype=jnp.int32)[None, :]
    ang = ((k * m) % N).astype(F32) * (2.0 * math.pi / N)
    valid = k <= P
    cos = jnp.where(valid, jnp.cos(ang), 0.0)
    sin = jnp.where(valid, jnp.sin(ang), 0.0)
    fwd = jnp.concatenate([cos, -sin], axis=0)
    ck = jnp.where((k == 0) | (k == P), 1.0, 2.0) / N
    inv = jnp.concatenate([(cos * ck)[:, :P].T, (-sin * ck)[:, :P].T], axis=1)
    return fwd.astype(BF16), inv.astype(BF16)


def _hy_filter_kernel(z_ref, w1_ref, b1_ref, w2_ref, b2_ref, fr_ref, w3_ref, dl_ref, o_ref, n_ref):
    z = z_ref[...]
    fr = fr_ref[...]
    hid = jnp.sin(fr * (jnp.dot(z, w1_ref[...], precision=HIGHEST, preferred_element_type=F32) + b1_ref[...]))
    hid = jnp.sin(fr * (jnp.dot(hid, w2_ref[...], precision=HIGHEST, preferred_element_type=F32) + b2_ref[...]))
    f = jnp.dot(hid, w3_ref[...], precision=HIGHEST, preferred_element_type=F32)
    f = f * jnp.exp(-z[:, 0:1] * dl_ref[...]) * z[:, HY_FEAT_VALID:HY_FEAT_VALID + 1]
    o_ref[...] = f

    @pl.when(pl.program_id(1) == 0)
    def _():
        n_ref[...] = jnp.zeros(n_ref.shape, F32)
    n_ref[...] += jnp.sum(jnp.abs(f), axis=0, keepdims=True)


HY_FEAT_VALID = 1 + 2 * HY_BANDS


def _hy_filters(L, w1, b1, w2, b2, w3, freq, tm, tn):
    i = jnp.arange(2 * L, dtype=jnp.int32)
    p = jnp.where(i < L, L - i, i - L).astype(F32)
    t = p / L
    bands = jnp.arange(1, HY_BANDS + 1, dtype=F32)
    ang = (2.0 * math.pi / L) * p[:, None] * bands[None, :]
    feats = jnp.concatenate([t[:, None], jnp.cos(ang), jnp.sin(ang), (i > 0).astype(F32)[:, None]], axis=-1)
    z = jnp.zeros((2 * L, LANES), F32).at[:, :HY_FEAT_VALID + 1].set(feats)
    w1p = jnp.zeros((LANES, HY_WIDTH), F32).at[:HY_FEAT_VALID].set(w1)
    deltas = jnp.abs(jnp.linspace(math.log(HY_TARGET) / HY_SLOW, math.log(HY_TARGET) / HY_FAST, D, dtype=F32))
    nj = D // tn
    n_anti = L // tm
    return pl.pallas_call(
        _hy_filter_kernel,
        grid=(HY_ORDER * nj, 2 * L // tm),
        in_specs=[pl.BlockSpec((tm, LANES), lambda j, i: (i, 0)),
                  pl.BlockSpec((LANES, HY_WIDTH), lambda j, i: (0, 0)),
                  pl.BlockSpec((1, HY_WIDTH), lambda j, i: (0, 0)),
                  pl.BlockSpec((HY_WIDTH, HY_WIDTH), lambda j, i: (0, 0)),
                  pl.BlockSpec((1, HY_WIDTH), lambda j, i: (0, 0)),
                  pl.BlockSpec((1, HY_WIDTH), lambda j, i: (0, 0)),
                  pl.BlockSpec((HY_WIDTH, tn),
                               lambda j, i: (0, (j // nj) * 2 * nj + jnp.where(i < n_anti, nj, 0) + j % nj)),
                  pl.BlockSpec((1, tn), lambda j, i: (0, j % nj))],
        out_specs=[pl.BlockSpec((tm, tn), lambda j, i: (i, j)),
                   pl.BlockSpec((1, tn), lambda j, i: (0, j))],
        out_shape=[jax.ShapeDtypeStruct((2 * L, HY_ORDER * D), F32),
                   jax.ShapeDtypeStruct((1, HY_ORDER * D), F32)],
        compiler_params=_cparams(2),
    )(z, w1p, b1.reshape(1, -1), w2, b2.reshape(1, -1), freq.reshape(1, -1), w3, deltas.reshape(1, D))


def _bmm_kernel(*refs, n_pair, has_scale, has_gate):
    it = iter(refs)
    a_refs = [next(it) for _ in range(n_pair)]
    x_refs = [next(it) for _ in range(n_pair)]
    sc_ref = next(it) if has_scale else None
    if has_gate:
        g_ref, z_ref, zb_ref = next(it), next(it), next(it)
    o_ref = next(it)
    acc = None
    for a, x in zip(a_refs, x_refs):
        p = jnp.dot(a[...], x[...].astype(BF16), preferred_element_type=F32)
        acc = p if acc is None else acc + p
    if has_scale:
        acc = acc / sc_ref[...]
    if has_gate:
        acc = g_ref[...] * (acc + z_ref[...] * zb_ref[...])
    o_ref[...] = acc.astype(o_ref.dtype)


def _bmm(a_list, x_list, *, nblk, tr, tn, out_rows, out_cols, out_dtype, out_nblk=None, out_boff=0,
         colscale=None, gate=None, zin=None, zbias=None, o_prev=None):
    out_nblk = nblk if out_nblk is None else out_nblk
    in_specs, args = [], []
    for a, cb, kb in a_list:
        in_specs.append(pl.BlockSpec((tr, kb), lambda b, i, j, cb=cb: (i, cb)))
        args.append(a)
    for (x, boff, coff), (_, _, kb) in zip(x_list, a_list):
        in_specs.append(pl.BlockSpec((None, kb, tn), lambda b, i, j, boff=boff, coff=coff: (b + boff, 0, coff + j)))
        args.append(x)
    if colscale is not None:
        in_specs.append(pl.BlockSpec((1, tn), lambda b, i, j: (0, j)))
        args.append(colscale)
    if gate is not None:
        for arr, boff, coff in (gate, zin):
            in_specs.append(pl.BlockSpec((None, tr, tn),
                                         lambda b, i, j, boff=boff, coff=coff: (b + boff, i, coff + j)))
            args.append(arr)
        in_specs.append(pl.BlockSpec((1, tn), lambda b, i, j: (0, j)))
        args.append(zbias)
    aliases = {}
    n_real = len(args)
    if o_prev is not None:
        in_specs.append(pl.BlockSpec(memory_space=pl.ANY))
        args.append(o_prev)
        aliases = {n_real: 0}

    def kern(*refs):
        refs = list(refs)
        if o_prev is not None:
            refs.pop(n_real)
        _bmm_kernel(*refs, n_pair=len(a_list), has_scale=colscale is not None, has_gate=gate is not None)

    return pl.pallas_call(
        kern,
        grid=(nblk, out_rows // tr, out_cols // tn),
        in_specs=in_specs,
        out_specs=pl.BlockSpec((None, tr, tn), lambda b, i, j: (b + out_boff, i, j)),
        out_shape=jax.ShapeDtypeStruct((out_nblk, out_rows, out_cols), out_dtype),
        input_output_aliases=aliases,
        compiler_params=_cparams(3),
    )(*args)


def _hy_mac_kernel(ure_ref, uim_ref, gre_ref, gim_ref, yre_ref, yim_ref, *, nb):
    for i in range(nb):
        acc_re = acc_im = None
        for j in range(nb):
            dd = i - j + nb - 1
            ur, ui = ure_ref[j], uim_ref[j]
            gr, gi = gre_ref[dd], gim_ref[dd]
            re = gr * ur - gi * ui
            im = gr * ui + gi * ur
            acc_re = re if acc_re is None else acc_re + re
            acc_im = im if acc_im is None else acc_im + im
        yre_ref[i] = acc_re.astype(yre_ref.dtype)
        yim_ref[i] = acc_im.astype(yim_ref.dtype)


def _hy_mac(u, g, order, B, nb, Pp, tc):
    tr = 384 if Pp % 384 == 0 else (256 if Pp % 256 == 0 else Pp)
    nr = Pp // tr
    nd = 2 * nb - 1
    u4 = u.reshape(B, nb, 2 * Pp, D)
    ospec = pl.BlockSpec((None, nb, tr, tc), lambda i, j, b: (b, 0, i, j))
    oshape = jax.ShapeDtypeStruct((B, nb, Pp, D), BF16)
    yre, yim = pl.pallas_call(
        functools.partial(_hy_mac_kernel, nb=nb),
        grid=(nr, D // tc, B),
        in_specs=[pl.BlockSpec((None, nb, tr, tc), lambda i, j, b: (b, 0, i, j)),
                  pl.BlockSpec((None, nb, tr, tc), lambda i, j, b: (b, 0, i + nr, j)),
                  pl.BlockSpec((nd, tr, tc), lambda i, j, b: (0, i, order * (D // tc) + j)),
                  pl.BlockSpec((nd, tr, tc), lambda i, j, b: (0, i + nr, order * (D // tc) + j))],
        out_specs=[ospec, ospec],
        out_shape=[oshape, oshape],
        compiler_params=_cparams(3),
    )(u4, u4, g, g)
    return yre.reshape(B * nb, Pp, D), yim.reshape(B * nb, Pp, D)


def _hyena_mixer(lay, h, x, mgate, w_in, conv_w, conv_b, f_w1, f_b1, f_w2, f_b2, f_w3, f_freq, f_bias,
                 w_out, tm):
    M = _M(lay)
    proj = _linear(h, [w_in], tm=tm, tn=1024)
    cv = _hy_conv(lay, proj, conv_w, conv_b, min(tm, lay.L0, 512), 512)
    tn = 512
    nj = D // tn
    z_all = None
    for seg in (0, 1):
        B, L, row_off = (lay.B0, lay.L0, 0) if seg == 0 else (lay.B1, lay.L1, _R0(lay))
        P, nb, Pp = _hy_geom(L)
        assert row_off % P == 0
        boff = row_off // P
        fwd, inv = _dft_mats(P)
        fext, nrm = _hy_filters(L, f_w1, f_b1, f_w2, f_b2, f_w3, f_freq, min(L, 512), tn)
        fx = fext.reshape(2 * nb, P, HY_ORDER * D)
        g = _bmm([(fwd, 0, P), (fwd, 1, P)], [(fx, 1, 0), (fx, 0, 0)], nblk=2 * nb - 1, tr=Pp, tn=tn,
                 out_rows=2 * Pp, out_cols=HY_ORDER * D, out_dtype=F32, colscale=nrm)
        cvv = cv.reshape(M // P, P, 3 * D)
        zsrc = (cvv, boff, 2 * nj)
        for n in range(HY_ORDER):
            u = _bmm([(fwd, 0, P)], [zsrc], nblk=B * nb, tr=Pp, tn=tn, out_rows=2 * Pp, out_cols=D,
                     out_dtype=F32)
            yre, yim = _hy_mac(u, g, n, B, nb, Pp, 256)
            last = n == HY_ORDER - 1
            z = _bmm([(inv, 0, Pp), (inv, 1, Pp)], [(yre, 0, 0), (yim, 0, 0)], nblk=B * nb, tr=min(P, 512),
                     tn=tn, out_rows=P, out_cols=D, out_dtype=F32,
                     out_nblk=(M // P) if last else None, out_boff=boff if last else 0,
                     gate=(cvv, boff, n * nj), zin=zsrc, zbias=f_bias[n].reshape(1, D),
                     o_prev=None if (not last or z_all is None) else z_all.reshape(M // P, P, D))
            zsrc = (z, boff if last else 0, 0)
        z_all = z.reshape(M, D)
    return _linear(z_all, [w_out], tm=tm, tn=1024, res=x, modgate=mgate, lay=lay)


def _rw_shift_kernel(x_ref, xp_ref, xn_ref, mu_ref, o_ref, *, lay, tm):
    j = pl.program_id(1)
    is_p, pos0, _ = _seq_pos(lay, pl.program_id(0) * tm)
    x = x_ref[...]
    hp = xp_ref[...]
    hn = xn_ref[...]
    rows = lax.broadcasted_iota(jnp.int32, (tm, 1), 0)
    pos = pos0 + rows
    col = pos & (GRID_W - 1)
    x_m1 = jnp.where(rows == 0, hp[GRID_W - 1:GRID_W], pltpu.roll(x, 1, 0))
    x_p1 = jnp.where(rows == tm - 1, hn[0:1], pltpu.roll(x, tm - 1, 0))
    if tm > GRID_W:
        x_mw = jnp.concatenate([hp, x[:tm - GRID_W]], axis=0)
        x_pw = jnp.concatenate([x[GRID_W:], hn], axis=0)
    else:
        x_mw, x_pw = hp, hn
    ok_m1 = jnp.where(is_p, pos, col) != 0
    ok_p1 = jnp.where(is_p, pos - (lay.L0 - 1), col - (GRID_W - 1)) != 0
    s_m1 = jnp.where(ok_m1, x_m1, 0.0)
    s_p1 = jnp.where(ok_p1, x_p1, 0.0)
    s_mw = jnp.where(pos >= GRID_W, x_mw, 0.0)
    s_pw = jnp.where(pos < lay.L1 - GRID_W, x_pw, 0.0)
    n_m1 = jnp.where(is_p, 2, 1)
    n_1 = jnp.where(is_p, 4, 2)
    sh = jnp.where(j < n_m1, s_m1, jnp.where(j < n_1, s_p1, jnp.where(j == 2, s_mw, s_pw)))
    xx = sh - x
    mu = mu_ref[...]
    for k in range(6):
        o_ref[k] = (x + xx * mu[k:k + 1]).astype(o_ref.dtype)


def _rw_shift(lay, h, mu, tm):
    M = h.shape[0]
    tc = D // 4
    g = GRID_W
    return pl.pallas_call(
        functools.partial(_rw_shift_kernel, lay=lay, tm=tm),
        grid=(M // tm, 4),
        in_specs=[pl.BlockSpec((tm, tc), lambda i, j: (i, j)),
                  pl.BlockSpec((g, tc), lambda i, j: (jnp.maximum(i * (tm // g) - 1, 0), j)),
                  pl.BlockSpec((g, tc), lambda i, j: (jnp.minimum((i + 1) * (tm // g), M // g - 1), j)),
                  pl.BlockSpec((6, tc), lambda i, j: (0, j))],
        out_specs=pl.BlockSpec((6, tm, tc), lambda i, j: (0, i, j)),
        out_shape=jax.ShapeDtypeStruct((6, M, D), BF16),
        compiler_params=_cparams(2),
    )(h, h, h, mu)


def _head_sum(x, seg):
    hi = x.astype(BF16)
    lo = (x - hi.astype(F32)).astype(BF16)
    return (jnp.dot(hi, seg, preferred_element_type=F32) + jnp.dot(lo, seg, preferred_element_type=F32))


def _head_seg(tc):
    r = lax.broadcasted_iota(jnp.int32, (tc, tc), 0) // RW_HEAD
    c = lax.broadcasted_iota(jnp.int32, (tc, tc), 1) // RW_HEAD
    return (r == c).astype(BF16)


def _rw_prep_kernel(r_ref, k_ref, v_ref, lw0_ref, lw1_ref, ar0_ref, ar1_ref, w0_ref, a0_ref, kk_p_ref,
                    ka_ref, rk_ref, kk_ref, w0o_ref, w1o_ref, kd0_ref, kd1_ref, b0_ref, b1_ref, bon_ref, *, tc):
    seg = _head_seg(tc)
    r, k, v = r_ref[...], k_ref[...], v_ref[...]
    kk = k * kk_p_ref[...]
    kk = kk * lax.rsqrt(_head_sum(kk * kk, seg) + 1e-12)
    kk_ref[...] = kk
    bonus = None
    for d, (lw_ref, ar_ref, wo_ref, kdo_ref, bo_ref) in enumerate(
            ((lw0_ref, ar0_ref, w0o_ref, kd0_ref, b0_ref), (lw1_ref, ar1_ref, w1o_ref, kd1_ref, b1_ref))):
        wo_ref[...] = -jnp.exp(-_softplus(-(w0_ref[d:d + 1, :] + lw_ref[...])) - 0.5)
        a = _sigmoid(a0_ref[d:d + 1, :] + ar_ref[...])
        kd = k * (1.0 + (a - 1.0) * ka_ref[...])
        kdo_ref[...] = kd
        bo_ref[...] = a * kk
        bon = _head_sum(r * kd * rk_ref[...], seg) * v
        bonus = bon if bonus is None else bonus + bon
    bon_ref[...] = bonus


def _rw_prep(r, k, v, lw0, lw1, ar0, ar1, w0, a0, k_k, k_a, r_k, tm, tc):
    M = r.shape[0]
    big = pl.BlockSpec((tm, tc), lambda i, j: (i, j))
    two = pl.BlockSpec((2, tc), lambda i, j: (0, j))
    one = pl.BlockSpec((1, tc), lambda i, j: (0, j))
    shape = jax.ShapeDtypeStruct((M, D), F32)
    return pl.pallas_call(
        functools.partial(_rw_prep_kernel, tc=tc),
        grid=(M // tm, D // tc),
        in_specs=[big] * 7 + [two, two, one, one, one],
        out_specs=[big] * 8,
        out_shape=[shape] * 8,
        compiler_params=_cparams(2),
    )(r, k, v, lw0, lw1, ar0, ar1, w0, a0, k_k.reshape(1, D), k_a.reshape(1, D), r_k.reshape(1, D))


def _rw_scan_kernel(*refs, reverse, T, Hb, zero_init, emit_state, nT):
    it = iter(refs)
    r_ref, w_ref, k_ref, v_ref, kk_ref, b_ref = (next(it) for _ in range(6))
    s0_ref = None if zero_init else next(it)
    y_ref = next(it)
    so_ref = next(it) if emit_state else None
    s_scr = next(it)
    C, Kd = CHUNK, RW_HEAD
    t = pl.program_id(2)

    @pl.when(t == 0)
    def _():
        for hh in range(Hb):
            if zero_init:
                s_scr[hh] = jnp.zeros((Kd, Kd), F32)
            else:
                s_scr[hh] = s0_ref[hh]

    row = lax.broadcasted_iota(jnp.int32, (C, C), 0)
    col = lax.broadcasted_iota(jnp.int32, (C, C), 1)
    incl = (row <= col) if reverse else (row >= col)
    strict = (row < col) if reverse else (row > col)
    tri = incl.astype(F32)
    last_i = 0 if reverse else C - 1
    nc = T // C
    nt_dims = (((1,), (1,)), ((), ()))
    tn_dims = (((0,), (0,)), ((), ()))

    def mm(a, b):
        return jnp.dot(a.astype(BF16), b.astype(BF16), preferred_element_type=F32)

    def body(i, carry):
        cc = (nc - 1 - i) if reverse else i
        rows = pl.ds(pl.multiple_of(cc * C, C), C)
        for hh in range(Hb):
            cs = slice(hh * Kd, (hh + 1) * Kd)
            lw = w_ref[rows, cs]
            r_, k_, v_ = r_ref[rows, cs], k_ref[rows, cs], v_ref[rows, cs]
            kk_, b_ = kk_ref[rows, cs], b_ref[rows, cs]
            cum = jnp.dot(tri, lw, precision=HIGHEST, preferred_element_type=F32)
            tot = cum[last_i:last_i + 1, :]
            e_neg = jnp.exp(-cum)
            e_rem = jnp.exp(tot - cum)
            alpha = kk_ * jnp.exp(cum - lw)
            rho = r_ * jnp.exp(cum)
            ar = jnp.concatenate([alpha, rho], axis=0).astype(BF16)
            bk = jnp.concatenate([b_ * e_neg, k_ * e_neg], axis=0).astype(BF16)
            big = lax.dot_general(ar, bk, nt_dims, preferred_element_type=F32)
            l_ab = jnp.where(strict, big[:C, :C], 0.0)
            l_ak = jnp.where(strict, big[:C, C:], 0.0)
            m_rb = jnp.where(incl, big[C:, :C], 0.0)
            m_rk = jnp.where(incl, big[C:, C:], 0.0)
            s = s_scr[hh]
            a_s = lax.dot_general(ar, s.astype(BF16), nt_dims, preferred_element_type=F32)
            rhs = a_s[:C] + mm(l_ak, v_)
            xp = -l_ab
            pows = [xp]
            for _ in range(5):
                xp = mm(xp, xp)
                pows.append(xp)
            u = rhs
            for xp in reversed(pows):
                u = u + mm(xp, u)
            y_ref[rows, cs] = a_s[C:] + mm(m_rk, v_) - mm(m_rb, u)
            vu = jnp.concatenate([v_, u], axis=0).astype(BF16)
            kb = jnp.concatenate([k_ * e_rem, -(b_ * e_rem)], axis=0).astype(BF16)
            s_scr[hh] = s * jnp.exp(tot) + lax.dot_general(vu, kb, tn_dims, preferred_element_type=F32)
        return carry

    lax.fori_loop(0, nc, body, 0)

    if emit_state:
        @pl.when(t == nT - 1)
        def _():
            for hh in range(Hb):
                so_ref[hh] = s_scr[hh]


def _rw_scan(lay, seg, direction, r, lw, kd, v, kk, b, s0, y_prev):
    M = _M(lay)
    Hb = 4
    H = RW_HEADS
    if seg == 0:
        B, L, boff = lay.B0, lay.L0, 0
    else:
        B, L, boff = lay.B1, lay.L1, _R0(lay) // lay.L1
    reverse = direction == 1
    T = min(L, 512)
    nT = L // T
    zero_init = s0 is None
    emit_state = seg == 0
    wd = Hb * RW_HEAD

    def tt(t):
        return (nT - 1 - t) if reverse else t

    spec = pl.BlockSpec((None, T, wd), lambda bb, h, t: (bb + boff, tt(t), h))
    args = [a.reshape(M // L, L, D) for a in (r, lw, kd, v, kk, b)]
    in_specs = [spec] * 6
    if not zero_init:
        in_specs.append(pl.BlockSpec((None, None, Hb, RW_HEAD, RW_HEAD),
                                     lambda bb, h, t: (bb, direction, h, 0, 0)))
        args.append(s0)
    aliases = {}
    n_real = len(args)
    if y_prev is not None:
        in_specs.append(pl.BlockSpec(memory_space=pl.ANY))
        args.append(y_prev.reshape(M // L, L, D))
        aliases = {n_real: 0}
    out_specs = [spec]
    out_shape = [jax.ShapeDtypeStruct((M // L, L, D), F32)]
    if emit_state:
        out_specs.append(pl.BlockSpec((None, Hb, RW_HEAD, RW_HEAD), lambda bb, h, t: (bb, h, 0, 0)))
        out_shape.append(jax.ShapeDtypeStruct((B, H, RW_HEAD, RW_HEAD), F32))

    def kern(*refs):
        refs = list(refs)
        if y_prev is not None:
            refs.pop(n_real)
        _rw_scan_kernel(*refs, reverse=reverse, T=T, Hb=Hb, zero_init=zero_init, emit_state=emit_state, nT=nT)

    outs = pl.pallas_call(
        kern,
        grid=(B, H // Hb, nT),
        in_specs=in_specs,
        out_specs=out_specs,
        out_shape=out_shape,
        scratch_shapes=[pltpu.VMEM((Hb, RW_HEAD, RW_HEAD), F32)],
        input_output_aliases=aliases,
        compiler_params=_cparams(3),
    )(*args)
    return outs[0].reshape(M, D), (outs[1] if emit_state else None)


def _rw_post_kernel(y0_ref, y1_ref, bon_ref, g_ref, lnw_ref, lnb_ref, o_ref, *, tc):
    seg = _head_seg(tc)
    y = y0_ref[...] + y1_ref[...]
    mean = _head_sum(y, seg) * (1.0 / RW_HEAD)
    yc = y - mean
    var = _head_sum(yc * yc, seg) * (1.0 / RW_HEAD)
    y = yc * lax.rsqrt(var + RW_GN_EPS) * lnw_ref[...] + lnb_ref[...]
    o_ref[...] = ((y + bon_ref[...]) * g_ref[...]).astype(o_ref.dtype)


def _rw_post(y0, y1, bonus, gate, ln_w, ln_b, tm, tc):
    M = y0.shape[0]
    big = pl.BlockSpec((tm, tc), lambda i, j: (i, j))
    one = pl.BlockSpec((1, tc), lambda i, j: (0, j))
    return pl.pallas_call(
        functools.partial(_rw_post_kernel, tc=tc),
        grid=(M // tm, D // tc),
        in_specs=[big] * 4 + [one, one],
        out_specs=big,
        out_shape=jax.ShapeDtypeStruct((M, D), BF16),
        compiler_params=_cparams(2),
    )(y0, y1, bonus, gate, ln_w.reshape(1, D), ln_b.reshape(1, D))


def _rwkv7_mixer(lay, h, x, mgate, s0, mu, w_r, w_k, w_v, w_o, w0, w1, w2, a0, a1, a2, g1, g2,
                 k_k, k_a, r_k, ln_w, ln_b, tm):
    x6 = _rw_shift(lay, h, mu, min(lay.L0, 256))
    r = _linear(x6, [w_r], x_idx=0, tm=tm, tn=1024)
    k = _linear(x6, [w_k], x_idx=2, tm=tm, tn=1024)
    v = _linear(x6, [w_v], x_idx=3, tm=tm, tn=1024)
    gh = _linear(x6, [g1], x_idx=5, tm=tm, tn=256, act="sigmoid", out_dtype=BF16)
    gate = _linear(gh, [g2], tm=tm, tn=1024)
    rk = RW_RANK

    def cat_in(w):
        out = jnp.zeros((D, 2 * LANES), F32)
        return out.at[:, :rk].set(w[0]).at[:, LANES:LANES + rk].set(w[1])

    def pad_out(w, d):
        return jnp.zeros((2 * LANES, D), F32).at[d * LANES:d * LANES + rk].set(w)

    tw = _linear(x6, [cat_in(w1)], x_idx=1, tm=tm, tn=2 * LANES, act="tanh", out_dtype=BF16)
    ta = _linear(x6, [cat_in(a1)], x_idx=4, tm=tm, tn=2 * LANES)
    lws = [_linear(tw, [pad_out(w2[d], d)], tm=tm, tn=1024) for d in range(2)]
    ars = [_linear(ta, [pad_out(a2[d], d)], tm=tm, tn=1024) for d in range(2)]
    kk, lw0, lw1, kd0, kd1, b0, b1, bonus = _rw_prep(r, k, v, lws[0], lws[1], ars[0], ars[1], w0, a0,
                                                     k_k, k_a, r_k, min(tm, 512), 256)
    ys, states = [], []
    for d, (lw, kd, b) in enumerate(((lw0, kd0, b0), (lw1, kd1, b1))):
        y, st = _rw_scan(lay, 0, d, r, lw, kd, v, kk, b, None, None)
        y, _ = _rw_scan(lay, 1, d, r, lw, kd, v, kk, b, s0, y)
        ys.append(y)
        states.append(st)
    yo = _rw_post(ys[0], ys[1], bonus, gate, ln_w, ln_b, min(tm, 512), 256)
    x = _linear(yo, [w_o], tm=tm, tn=1024, res=x, modgate=mgate, lay=lay)
    return x, jnp.stack(states, axis=1)


def _router_kernel(x_ref, w_ref, o_ref, *, n_exp):
    logits = jnp.dot(x_ref[...].astype(F32), w_ref[...], precision=HIGHEST, preferred_element_type=F32)
    lane = lax.broadcasted_iota(jnp.int32, logits.shape, 1)
    neg = -jnp.inf
    lg = jnp.where(lane < n_exp, logits, neg)
    m1 = jnp.max(lg, axis=-1, keepdims=True)
    i1 = jnp.min(jnp.where(lg == m1, lane, LANES), axis=-1, keepdims=True)
    lg2 = jnp.where(lane == i1, neg, lg)
    m2 = jnp.max(lg2, axis=-1, keepdims=True)
    i2 = jnp.min(jnp.where(lg2 == m2, lane, LANES), axis=-1, keepdims=True)
    e = jnp.exp(m2 - m1)
    p1 = 1.0 / (1.0 + e)
    o_ref[...] = jnp.where(lane == i1, p1, 0.0) + jnp.where(lane == i2, e * p1, 0.0)


def _router(h, router, tm):
    M = h.shape[0]
    n_exp = router.shape[1]
    wp = jnp.zeros((D, LANES), F32).at[:, :n_exp].set(router)
    return pl.pallas_call(
        functools.partial(_router_kernel, n_exp=n_exp),
        grid=(M // tm,),
        in_specs=[pl.BlockSpec((tm, D), lambda i: (i, 0)), pl.BlockSpec((D, LANES), lambda i: (0, 0))],
        out_specs=pl.BlockSpec((tm, LANES), lambda i: (i, 0)),
        out_shape=jax.ShapeDtypeStruct((M, LANES), F32),
        compiler_params=_cparams(1),
    )(h, wp)


def _moe(lay, h, x, mgate, router, w1, w3, w2, e0, n_exp, tm):
    gates = _router(h, router, min(tm, 512))
    for e in range(n_exp):
        a = _linear(h, [w1, w3], w_idx=e0 + e, tm=tm, tn=256, swiglu=True, out_dtype=BF16)
        x = _linear(a, [w2], w_idx=e0 + e, tm=tm, tn=1024, res=x, modgate=mgate, lay=lay,
                    rowgate=gates, rowgate_col=e)
    return x


def kernel(x_prompt, x_sample, state_l0_gla, state_l2_hgrn2, state_l3_rwkv7, c, c_ctx, norm_mix_g, norm_ffn_g, mod_w, mod_b, gla_w_in, gla_gk_w1, gla_gk_w2, gla_gk_b, gla_norm_g, gla_w_out, hy_w_in, hy_conv_w, hy_conv_b, hy_f_w1, hy_f_b1, hy_f_w2, hy_f_b2, hy_f_w3, hy_f_freq, hy_f_bias, hy_w_out, hg_w_in, hg_lb, hg_norm_g, hg_w_out, rw_mu, rw_w_r, rw_w_k, rw_w_v, rw_w_o, rw_w0, rw_w1, rw_w2, rw_a0, rw_a1, rw_a2, rw_g1, rw_g2, rw_k_k, rw_k_a, rw_r_k, rw_ln_w, rw_ln_b, ffn_w1, ffn_w3, ffn_w2, moe_router, moe_w1, moe_w3, moe_w2, final_norm_g):
    lay = Lay(x_prompt.shape[0], x_prompt.shape[1], x_sample.shape[0], x_sample.shape[1])
    M = _M(lay)
    r0 = _R0(lay)
    tm = _row_tile(lay, 1024)
    tm_norm = _row_tile(lay, 512)
    x = jnp.concatenate([x_prompt.reshape(-1, D), x_sample.reshape(-1, D)], axis=0)
    cond = jnp.zeros((8, D), F32).at[0].set(c_ctx).at[1:1 + lay.B1].set(c)
    n_exp = moe_w1.shape[1]
    moe_w1r = moe_w1.reshape((-1,) + moe_w1.shape[2:])
    moe_w3r = moe_w3.reshape((-1,) + moe_w3.shape[2:])
    moe_w2r = moe_w2.reshape((-1,) + moe_w2.shape[2:])
    new_states = {}
    for l in range(DEPTH):
        m = _linear(cond, [mod_w], w_idx=l, tm=8, tn=1024, bias=mod_b[l], pre_silu=True)
        mods = [m[:, k * D:(k + 1) * D].reshape(8, 1, D) for k in range(6)]
        kind = l % 4
        h = _norm(x, norm_mix_g[l], lay=lay, tm=tm_norm, out_dtype=F32 if kind == 3 else BF16,
                  shift=mods[0], scale=mods[1])
        if kind == 0:
            x, new_states[0] = _gla_mixer(lay, h, x, mods[2], state_l0_gla, gla_w_in, gla_gk_w1, gla_gk_w2,
                                          gla_gk_b, gla_norm_g, gla_w_out, tm)
        elif kind == 1:
            x = _hyena_mixer(lay, h, x, mods[2], hy_w_in, hy_conv_w, hy_conv_b, hy_f_w1, hy_f_b1, hy_f_w2,
                             hy_f_b2, hy_f_w3, hy_f_freq, hy_f_bias, hy_w_out, tm)
        elif kind == 2:
            x, new_states[2] = _hgrn2_mixer(lay, h, x, mods[2], state_l2_hgrn2, l, hg_w_in, hg_lb, hg_norm_g,
                                            hg_w_out, tm)
        else:
            x, new_states[3] = _rwkv7_mixer(lay, h, x, mods[2], state_l3_rwkv7, rw_mu, rw_w_r, rw_w_k, rw_w_v,
                                            rw_w_o, rw_w0, rw_w1, rw_w2, rw_a0, rw_a1, rw_a2, rw_g1, rw_g2,
                                            rw_k_k, rw_k_a, rw_r_k, rw_ln_w, rw_ln_b, tm)
        h = _norm(x, norm_ffn_g[l], lay=lay, tm=tm_norm, out_dtype=BF16, shift=mods[3], scale=mods[4])
        j = l // 2
        if l % 2 == 0:
            a = _linear(h, [ffn_w1, ffn_w3], w_idx=j, tm=tm, tn=512, swiglu=True, out_dtype=BF16)
            x = _linear(a, [ffn_w2], w_idx=j, tm=tm, tn=512, res=x, modgate=mods[5], lay=lay)
        else:
            x = _moe(lay, h, x, mods[5], moe_router[j], moe_w1r, moe_w3r, moe_w2r, j * n_exp, n_exp, tm)
    yp = _norm(x, final_norm_g, lay=lay, tm=tm_norm, out_dtype=F32, row0=0, nrows=r0)
    ys = _norm(x, final_norm_g, lay=lay, tm=tm_norm, out_dtype=F32, row0=r0, nrows=M - r0)
    return (yp.reshape(x_prompt.shape), ys.reshape(x_sample.shape), new_states[0], new_states[2],
            new_states[3])
```

```python
import collections
import functools
import math

import jax
import jax.numpy as jnp
from jax import lax
from jax.experimental import pallas as pl
from jax.experimental.pallas import tpu as pltpu

F32 = jnp.float32
BF16 = jnp.bfloat16
HIGHEST = lax.Precision.HIGHEST

D = 2048
DEPTH = 4
CHUNK = 64
EPS = 1e-6
GRID_W = 64
GLA_HEADS, GLA_DK, GLA_DV = 4, 256, 512
GLA_GATE_RANK, GLA_GATE_NORM = 16, 16.0
HY_ORDER, HY_BANDS, HY_WIDTH = 2, 16, 64
HY_FAST, HY_SLOW, HY_TARGET = 0.3, 1.5, 1e-2
HG_HEADS, HG_DK, HG_DV = 16, 128, 128
RW_HEADS, RW_HEAD = 32, 64
RW_RANK = 96
RW_GN_EPS = 64e-5
N_EXPERTS = 8
LANES = 128
VMEM_LIMIT_MB = 56
LINEAR_VMEM_BUDGET = 44 * 1024 * 1024
HY_BLOCK_MAX = 1024
RW_SCAN_HEADS = 8
RW_SCAN_ROWS = 256
GLA_SCAN_ROWS = 512

Lay = collections.namedtuple("Lay", "B0 L0 B1 L1")


def _R0(lay):
    return lay.B0 * lay.L0


def _M(lay):
    return lay.B0 * lay.L0 + lay.B1 * lay.L1


def _group_of_row(lay, r):
    r0 = _R0(lay)
    return jnp.where(r < r0, 0, 1 + (r - r0) // lay.L1)


def _cparams(n_axes, vmem_mb=VMEM_LIMIT_MB):
    return pltpu.CompilerParams(dimension_semantics=("arbitrary",) * n_axes,
                                vmem_limit_bytes=vmem_mb * 1024 * 1024)


def _row_tile(lay, cap):
    t = cap
    while (_R0(lay) % t) or (lay.L1 % t) or (lay.L0 % t and t % lay.L0):
        t //= 2
    return t


def _sigmoid(x):
    return 1.0 / (1.0 + jnp.exp(-x))


def _softplus(x):
    return jnp.maximum(x, 0.0) + jnp.log(1.0 + jnp.exp(-jnp.abs(x)))


def _linear_kernel(*refs, n_w, cache_w, has_bias, has_res, has_mod, has_row, row_col, act,
                   pre_silu, swiglu, hp):
    it = iter(refs)
    x_ref = next(it)
    w_refs = [next(it) for _ in range(n_w)]
    b_ref = next(it) if has_bias else None
    res_ref = next(it) if has_res else None
    mod_ref = next(it) if has_mod else None
    row_ref = next(it) if has_row else None
    o_ref = next(it)
    scr = [next(it) for _ in range(n_w)] if cache_w else None

    x = x_ref[...]
    if pre_silu:
        x = x.astype(F32)
        x = x * _sigmoid(x)
    if hp:
        accs = [jnp.dot(x.astype(F32), w[...].astype(F32), precision=HIGHEST,
                        preferred_element_type=F32) for w in w_refs]
    else:
        if cache_w:
            @pl.when(pl.program_id(1) == 0)
            def _():
                for s, w in zip(scr, w_refs):
                    s[...] = w[...].astype(BF16)
            wv = [s[...] for s in scr]
        else:
            wv = [w[...].astype(BF16) for w in w_refs]
        xb = x.astype(BF16)
        accs = [jnp.dot(xb, w, preferred_element_type=F32) for w in wv]
    if swiglu:
        a = accs[0]
        acc = a * _sigmoid(a) * accs[1]
    else:
        acc = accs[0]
    if has_bias:
        acc = acc + b_ref[...]
    if act == "tanh":
        acc = jnp.tanh(acc)
    elif act == "sigmoid":
        acc = _sigmoid(acc)
    if has_row:
        acc = acc * row_ref[:, row_col:row_col + 1]
    if has_mod:
        acc = acc * mod_ref[...]
    if has_res:
        acc = res_ref[...] + acc
    o_ref[...] = acc.astype(o_ref.dtype)


def _linear(x, ws, *, tm, tn, out_dtype=F32, w_idx=None, x_idx=None, bias=None, act=None,
            pre_silu=False, swiglu=False, res=None, modgate=None, lay=None, rowgate=None,
            rowgate_col=0, hp=False):
    n_w = len(ws)
    M, K = x.shape[-2:]
    N = ws[0].shape[-1]
    tm = min(tm, M)
    tn = min(tn, N)

    def vmem_bytes(tm_):
        cache = (not hp) and M // tm_ > 1 and ws[0].dtype != BF16
        b = 2 * tm_ * K * x.dtype.itemsize
        b += n_w * K * tn * (2 * ws[0].dtype.itemsize + (2 if cache else 0))
        b += 2 * tm_ * tn * jnp.dtype(out_dtype).itemsize + 3 * tm_ * tn * 4
        b += 2 * tm_ * tn * 4 if res is not None else 0
        b += 2 * tm_ * LANES * 4 if rowgate is not None else 0
        return b

    while vmem_bytes(tm) > LINEAR_VMEM_BUDGET and tm > 256:
        tm //= 2
    assert M % tm == 0 and N % tn == 0, (M, tm, N, tn)
    n_i = M // tm
    cache_w = (not hp) and n_i > 1 and ws[0].dtype != BF16

    if x_idx is None:
        x_spec = pl.BlockSpec((tm, K), lambda j, i: (i, 0))
    else:
        x_spec = pl.BlockSpec((None, tm, K), lambda j, i: (x_idx, i, 0))
    if w_idx is None:
        w_spec = pl.BlockSpec((K, tn), lambda j, i: (0, j))
    else:
        w_spec = pl.BlockSpec((None, K, tn), lambda j, i: (w_idx, 0, j))
    in_specs = [x_spec] + [w_spec] * n_w
    args = [x] + list(ws)
    if bias is not None:
        in_specs.append(pl.BlockSpec((1, tn), lambda j, i: (0, j)))
        args.append(bias.reshape(1, N).astype(F32))
    if res is not None:
        in_specs.append(pl.BlockSpec((tm, tn), lambda j, i: (i, j)))
        args.append(res)
    if modgate is not None:
        in_specs.append(pl.BlockSpec((None, 1, tn), lambda j, i: (_group_of_row(lay, i * tm), 0, j)))
        args.append(modgate)
    if rowgate is not None:
        in_specs.append(pl.BlockSpec((tm, LANES), lambda j, i: (i, 0)))
        args.append(rowgate)
    kern = functools.partial(
        _linear_kernel, n_w=n_w, cache_w=cache_w, has_bias=bias is not None, has_res=res is not None,
        has_mod=modgate is not None, has_row=rowgate is not None, row_col=rowgate_col, act=act,
        pre_silu=pre_silu, swiglu=swiglu, hp=hp)
    scratch = [pltpu.VMEM((K, tn), BF16) for _ in range(n_w)] if cache_w else []
    return pl.pallas_call(
        kern,
        grid=(N // tn, n_i),
        in_specs=in_specs,
        out_specs=pl.BlockSpec((tm, tn), lambda j, i: (i, j)),
        out_shape=jax.ShapeDtypeStruct((M, N), out_dtype),
        scratch_shapes=scratch,
        compiler_params=_cparams(2),
    )(*args)


def _norm_kernel(*refs, has_mod):
    if has_mod:
        x_ref, g_ref, sh_ref, sc_ref, o_ref = refs
    else:
        x_ref, g_ref, o_ref = refs
    x = x_ref[...]
    y = x * lax.rsqrt(jnp.mean(x * x, axis=-1, keepdims=True) + EPS) * g_ref[...]
    if has_mod:
        y = y * (1.0 + sc_ref[...]) + sh_ref[...]
    o_ref[...] = y.astype(o_ref.dtype)


def _norm(x, g, *, lay, tm, out_dtype, shift=None, scale=None, row0=0, nrows=None):
    nrows = x.shape[0] if nrows is None else nrows
    assert row0 % tm == 0 and nrows % tm == 0
    off = row0 // tm
    has_mod = shift is not None
    in_specs = [pl.BlockSpec((tm, D), lambda i: (i + off, 0)),
                pl.BlockSpec((1, D), lambda i: (0, 0))]
    args = [x, g.reshape(1, D)]
    if has_mod:
        mspec = pl.BlockSpec((None, 1, D), lambda i: (_group_of_row(lay, (i + off) * tm), 0, 0))
        in_specs += [mspec, mspec]
        args += [shift, scale]
    return pl.pallas_call(
        functools.partial(_norm_kernel, has_mod=has_mod),
        grid=(nrows // tm,),
        in_specs=in_specs,
        out_specs=pl.BlockSpec((tm, D), lambda i: (i, 0)),
        out_shape=jax.ShapeDtypeStruct((nrows, D), out_dtype),
        compiler_params=_cparams(1),
    )(*args)


def _scan_kernel(*refs, mode, reverse, T, Hb, dk, dv, zero_init, emit_state, nT):
    it = iter(refs)
    q_ref = next(it)
    k_ref = next(it) if mode == "gla" else None
    f_ref = next(it)
    v_ref = next(it)
    lb_ref = next(it) if mode == "hgrn" else None
    s0_ref = None if zero_init else next(it)
    o_ref = next(it)
    so_ref = next(it) if emit_state else None
    s_scr = next(it)
    C = CHUNK
    t = pl.program_id(2)

    @pl.when(t == 0)
    def _():
        for hh in range(Hb):
            if zero_init:
                s_scr[hh] = jnp.zeros((dv, dk), F32)
            else:
                s_scr[hh] = s0_ref[hh].T

    row = lax.broadcasted_iota(jnp.int32, (C, C), 0)
    col = lax.broadcasted_iota(jnp.int32, (C, C), 1)
    incl = (row <= col) if reverse else (row >= col)
    tri = incl.astype(F32)
    ref_i = (C - 1 - C // 2) if reverse else C // 2
    last_i = 0 if reverse else C - 1
    nc = T // C
    nt_dims = (((1,), (1,)), ((), ()))
    tn_dims = (((0,), (0,)), ((), ()))

    order = [(nc - 1 - i) if reverse else i for i in range(nc)]
    items = [(cc, hh) for cc in order for hh in range(Hb)]
    wide = {}
    for cc in order:
        rows = slice(cc * C, (cc + 1) * C)
        q = q_ref[rows, :]
        fx = f_ref[rows, :]
        if mode == "gla":
            k = k_ref[rows, :]
            logf = (jnp.minimum(fx, 0.0) - jnp.log(1.0 + jnp.exp(-jnp.abs(fx)))) * (1.0 / GLA_GATE_NORM)
            q = q * dk ** -0.5
        else:
            lb = lb_ref[...]
            f = lb + (1.0 - lb) * _sigmoid(fx)
            k = 1.0 - f
            logf = jnp.log(f)
            q = q * _sigmoid(q) * dk ** -0.5
        b = jnp.dot(tri, logf, precision=HIGHEST, preferred_element_type=F32)
        b_mid = b[ref_i:ref_i + 1, :]
        b_last = b[last_i:last_i + 1, :]
        wide[cc] = dict(qe=(q * jnp.exp(b - b_mid)).astype(BF16), ke=(k * jnp.exp(b_mid - b)).astype(BF16),
                        qi=(q * jnp.exp(b)).astype(BF16), ku=(k * jnp.exp(b_last - b)).astype(BF16),
                        dec=jnp.exp(b_last), v=v_ref[rows, :].astype(BF16))

    def head(cc, hh, name):
        w = dv if name == "v" else dk
        return wide[cc][name][:, hh * w:(hh + 1) * w]

    sc = [lax.dot_general(head(cc, hh, "qe"), head(cc, hh, "ke"), nt_dims, preferred_element_type=F32)
          for cc, hh in items]
    sc = [jnp.where(incl, s, 0.0).astype(BF16) for s in sc]
    o_intra = [jnp.dot(s, head(cc, hh, "v"), preferred_element_type=F32) for s, (cc, hh) in zip(sc, items)]
    upd = [lax.dot_general(head(cc, hh, "v"), head(cc, hh, "ku"), tn_dims, preferred_element_type=F32)
           for cc, hh in items]
    states = [s_scr[hh] for hh in range(Hb)]
    o_rows = [None] * nc
    for n, cc in enumerate(order):
        idx = [n * Hb + hh for hh in range(Hb)]
        o_heads = [o_intra[i] + lax.dot_general(head(cc, hh, "qi"), states[hh].astype(BF16), nt_dims,
                                                preferred_element_type=F32) for hh, i in enumerate(idx)]
        states = [states[hh] * head(cc, hh, "dec") + upd[i] for hh, i in enumerate(idx)]
        o_rows[cc] = jnp.concatenate(o_heads, axis=1) if Hb > 1 else o_heads[0]
    o_ref[...] = jnp.concatenate(o_rows, axis=0) if nc > 1 else o_rows[0]
    for hh in range(Hb):
        s_scr[hh] = states[hh]

    if emit_state:
        @pl.when(t == nT - 1)
        def _():
            for hh in range(Hb):
                so_ref[hh] = s_scr[hh].T


def _scan(mode, lay, seg, direction, proj, fsrc, lb, s0, o_prev):
    M = _M(lay)
    if mode == "gla":
        H, dk, dv, Hb = GLA_HEADS, GLA_DK, GLA_DV, 1
        q_c, k_c, f_c, v_c = 0, GLA_HEADS, direction * GLA_HEADS, 2 * GLA_HEADS * GLA_DK // GLA_DV
    else:
        H, dk, dv, Hb = HG_HEADS, HG_DK, HG_DV, 4
        nb = HG_HEADS // Hb
        q_c, k_c, f_c, v_c = 0, None, (1 + direction) * nb, 3 * nb
    if seg == 0:
        B, L, boff = lay.B0, lay.L0, 0
    else:
        B, L, boff = lay.B1, lay.L1, _R0(lay) // lay.L1
        assert _R0(lay) % lay.L1 == 0
    reverse = direction == 1
    T = min(L, GLA_SCAN_ROWS)
    nT = L // T
    zero_init = s0 is None
    emit_state = seg == 0
    W = proj.shape[1]
    pv = proj.reshape(M // L, L, W)
    fv = fsrc.reshape(M // L, L, fsrc.shape[1])

    def tt(t):
        return (nT - 1 - t) if reverse else t

    def cspec(width, c0):
        return pl.BlockSpec((None, T, width), lambda b, h, t: (b + boff, tt(t), c0 + h))

    in_specs = [cspec(Hb * dk, q_c)]
    args = [pv]
    if mode == "gla":
        in_specs.append(cspec(Hb * dk, k_c))
        args.append(pv)
    in_specs.append(cspec(Hb * dk, f_c))
    args.append(fv)
    in_specs.append(cspec(Hb * dv, v_c))
    args.append(pv)
    if mode == "hgrn":
        in_specs.append(pl.BlockSpec((None, 1, Hb * dk), lambda b, h, t: (direction, 0, h)))
        args.append(lb)
    if not zero_init:
        in_specs.append(pl.BlockSpec((None, None, Hb, dk, dv), lambda b, h, t: (b, direction, h, 0, 0)))
        args.append(s0)
    aliases = {}
    if o_prev is not None:
        in_specs.append(pl.BlockSpec(memory_space=pl.ANY))
        args.append(o_prev.reshape(M // L, L, H * dv))
        aliases = {len(args) - 1: 0}
    out_specs = [pl.BlockSpec((None, T, Hb * dv), lambda b, h, t: (b + boff, tt(t), h))]
    out_shape = [jax.ShapeDtypeStruct((M // L, L, H * dv), F32)]
    if emit_state:
        out_specs.append(pl.BlockSpec((None, Hb, dk, dv), lambda b, h, t: (b, h, 0, 0)))
        out_shape.append(jax.ShapeDtypeStruct((B, H, dk, dv), F32))

    def kern(*refs):
        refs = list(refs)
        if o_prev is not None:
            n_in = len(args)
            refs.pop(n_in - 1)
        _scan_kernel(*refs, mode=mode, reverse=reverse, T=T, Hb=Hb, dk=dk, dv=dv,
                     zero_init=zero_init, emit_state=emit_state, nT=nT)

    outs = pl.pallas_call(
        kern,
        grid=(B, H // Hb, nT),
        in_specs=in_specs,
        out_specs=out_specs,
        out_shape=out_shape,
        scratch_shapes=[pltpu.VMEM((Hb, dv, dk), F32)],
        input_output_aliases=aliases,
        compiler_params=_cparams(3),
    )(*args)
    o = outs[0].reshape(M, H * dv)
    return o, (outs[1] if emit_state else None)


def _gated_norm_kernel(of_ref, ob_ref, g_ref, gain_ref, o_ref, *, H, dv):
    gain = gain_ref[...]
    for h in range(H):
        cs = slice(h * dv, (h + 1) * dv)
        o = of_ref[:, cs] + ob_ref[:, cs]
        o = o * lax.rsqrt(jnp.mean(o * o, axis=-1, keepdims=True) + EPS) * gain
        g = g_ref[:, cs]
        o_ref[:, cs] = (o * (g * _sigmoid(g))).astype(o_ref.dtype)


def _gated_norm(o_f, o_b, proj, g_col, gain, H, dv, tm):
    M = o_f.shape[0]
    return pl.pallas_call(
        functools.partial(_gated_norm_kernel, H=H, dv=dv),
        grid=(M // tm,),
        in_specs=[pl.BlockSpec((tm, D), lambda i: (i, 0)),
                  pl.BlockSpec((tm, D), lambda i: (i, 0)),
                  pl.BlockSpec((tm, D), lambda i: (i, g_col)),
                  pl.BlockSpec((1, dv), lambda i: (0, 0))],
        out_specs=pl.BlockSpec((tm, D), lambda i: (i, 0)),
        out_shape=jax.ShapeDtypeStruct((M, D), BF16),
        compiler_params=_cparams(1),
    )(o_f, o_b, proj, gain.reshape(1, dv))


def _bidir_scan(mode, lay, proj, fsrc, lb, s0_sample):
    outs, states = [], []
    for d in range(2):
        o, st = _scan(mode, lay, 0, d, proj, fsrc, lb, None, None)
        o, _ = _scan(mode, lay, 1, d, proj, fsrc, lb, s0_sample, o)
        outs.append(o)
        states.append(st)
    return outs[0], outs[1], jnp.stack(states, axis=1)


def _gla_mixer(lay, h, x, mgate, s0, w_in, gk_w1, gk_w2, gk_b, norm_g, w_out, tm):
    kd = GLA_HEADS * GLA_DK
    proj = _linear(h, [w_in], tm=tm, tn=1024)
    w1cat = jnp.zeros((D, LANES), F32).at[:, :GLA_GATE_RANK].set(gk_w1[0])
    w1cat = w1cat.at[:, GLA_GATE_RANK:2 * GLA_GATE_RANK].set(gk_w1[1])
    w2cat = jnp.zeros((LANES, 2 * kd), F32).at[:GLA_GATE_RANK, :kd].set(gk_w2[0])
    w2cat = w2cat.at[GLA_GATE_RANK:2 * GLA_GATE_RANK, kd:].set(gk_w2[1])
    low = _linear(h, [w1cat], tm=tm, tn=LANES)
    gk = _linear(low, [w2cat], tm=tm, tn=1024, bias=gk_b.reshape(2 * kd))
    o_f, o_b, new_state = _bidir_scan("gla", lay, proj, gk, None, s0)
    y = _gated_norm(o_f, o_b, proj, 2, norm_g, GLA_HEADS, GLA_DV, min(tm, 256))
    x = _linear(y, [w_out], tm=tm, tn=1024, res=x, modgate=mgate, lay=lay)
    return x, new_state


def _hgrn2_mixer(lay, h, x, mgate, s0, layer_idx, w_in, lb_raw, norm_g, w_out, tm):
    proj = _linear(h, [w_in], tm=tm, tn=1024)
    lb = jnp.cumsum(jax.nn.softmax(lb_raw.astype(F32), axis=1), axis=1)
    lb = (lb - lb[:, :1])[:, layer_idx].reshape(2, 1, D)
    o_f, o_b, new_state = _bidir_scan("hgrn", lay, proj, proj, lb, s0)
    y = _gated_norm(o_f, o_b, proj, 4, norm_g, HG_HEADS, HG_DV, min(tm, 256))
    x = _linear(y, [w_out], tm=tm, tn=1024, res=x, modgate=mgate, lay=lay)
    return x, new_state


def _seq_pos(lay, r):
    r0 = _R0(lay)
    is_p = r < r0
    pos = jnp.where(is_p, r % lay.L0, (r - r0) % lay.L1)
    return is_p, pos, jnp.where(is_p, lay.L0, lay.L1)


def _hy_conv_kernel(x_ref, xp_ref, xn_ref, w_ref, b_ref, o_ref, *, lay, tm):
    _, pos0, seq_len = _seq_pos(lay, pl.program_id(0) * tm)
    x = x_ref[...]
    rows = lax.broadcasted_iota(jnp.int32, (tm, 1), 0)
    prev_row = jnp.where(pos0 == 0, 0.0, xp_ref[7:8, :])
    next_row = jnp.where(pos0 + tm == seq_len, 0.0, xn_ref[0:1, :])
    x_m1 = jnp.where(rows == 0, prev_row, pltpu.roll(x, 1, 0))
    x_p1 = jnp.where(rows == tm - 1, next_row, pltpu.roll(x, tm - 1, 0))
    w = w_ref[...]
    o_ref[...] = x_m1 * w[0:1] + x * w[1:2] + x_p1 * w[2:3] + b_ref[...]


def _hy_conv(lay, x, w, b, tm, tc):
    M, W = x.shape
    sub = 8
    return pl.pallas_call(
        functools.partial(_hy_conv_kernel, lay=lay, tm=tm),
        grid=(M // tm, W // tc),
        in_specs=[pl.BlockSpec((tm, tc), lambda i, j: (i, j)),
                  pl.BlockSpec((sub, tc), lambda i, j: (jnp.maximum(i * (tm // sub) - 1, 0), j)),
                  pl.BlockSpec((sub, tc), lambda i, j: (jnp.minimum((i + 1) * (tm // sub), M // sub - 1), j)),
                  pl.BlockSpec((3, tc), lambda i, j: (0, j)),
                  pl.BlockSpec((1, tc), lambda i, j: (0, j))],
        out_specs=pl.BlockSpec((tm, tc), lambda i, j: (i, j)),
        out_shape=jax.ShapeDtypeStruct((M, W), F32),
        compiler_params=_cparams(2),
    )(x, x, x, w, b.reshape(1, W))


def _hy_geom(L):
    P = min(L, HY_BLOCK_MAX)
    return P, L // P, P + LANES


def _dft_mats(P):
    N, Pp = 2 * P, P + LANES
    k = jnp.arange(Pp, dtype=jnp.int32)[:, None]
    m = jnp.arange(N, dtype=jnp.int32)[None, :]
    ang = ((k * m) % N).astype(F32) * (2.0 * math.pi / N)
    valid = k <= P
    cos = jnp.where(valid, jnp.cos(ang), 0.0)
    sin = jnp.where(valid, jnp.sin(ang), 0.0)
    fwd = jnp.concatenate([cos, -sin], axis=0)
    ck = jnp.where((k == 0) | (k == P), 1.0, 2.0) / N
    inv = jnp.concatenate([(cos * ck)[:, :P].T, (-sin * ck)[:, :P].T], axis=1)
    return fwd.astype(BF16), inv.astype(BF16)


def _hy_filter_kernel(z_ref, w1_ref, b1_ref, w2_ref, b2_ref, fr_ref, w3_ref, dl_ref, o_ref, n_ref):
    z = z_ref[...]
    fr = fr_ref[...]
    hid = jnp.sin(fr * (jnp.dot(z, w1_ref[...], precision=HIGHEST, preferred_element_type=F32) + b1_ref[...]))
    hid = jnp.sin(fr * (jnp.dot(hid, w2_ref[...], precision=HIGHEST, preferred_element_type=F32) + b2_ref[...]))
    f = jnp.dot(hid, w3_ref[...], precision=HIGHEST, preferred_element_type=F32)
    f = f * jnp.exp(-z[:, 0:1] * dl_ref[...]) * z[:, HY_FEAT_VALID:HY_FEAT_VALID + 1]
    o_ref[...] = f

    @pl.when(pl.program_id(1) == 0)
    def _():
        n_ref[...] = jnp.zeros(n_ref.shape, F32)
    n_ref[...] += jnp.sum(jnp.abs(f), axis=0, keepdims=True)


HY_FEAT_VALID = 1 + 2 * HY_BANDS


def _hy_filters(L, w1, b1, w2, b2, w3, freq, tm, tn):
    i = jnp.arange(2 * L, dtype=jnp.int32)
    p = jnp.where(i < L, L - i, i - L).astype(F32)
    t = p / L
    bands = jnp.arange(1, HY_BANDS + 1, dtype=F32)
    ang = (2.0 * math.pi / L) * p[:, None] * bands[None, :]
    feats = jnp.concatenate([t[:, None], jnp.cos(ang), jnp.sin(ang), (i > 0).astype(F32)[:, None]], axis=-1)
    z = jnp.zeros((2 * L, LANES), F32).at[:, :HY_FEAT_VALID + 1].set(feats)
    w1p = jnp.zeros((LANES, HY_WIDTH), F32).at[:HY_FEAT_VALID].set(w1)
    deltas = jnp.abs(jnp.linspace(math.log(HY_TARGET) / HY_SLOW, math.log(HY_TARGET) / HY_FAST, D, dtype=F32))
    nj = D // tn
    n_anti = L // tm
    return pl.pallas_call(
        _hy_filter_kernel,
        grid=(HY_ORDER * nj, 2 * L // tm),
        in_specs=[pl.BlockSpec((tm, LANES), lambda j, i: (i, 0)),
                  pl.BlockSpec((LANES, HY_WIDTH), lambda j, i: (0, 0)),
                  pl.BlockSpec((1, HY_WIDTH), lambda j, i: (0, 0)),
                  pl.BlockSpec((HY_WIDTH, HY_WIDTH), lambda j, i: (0, 0)),
                  pl.BlockSpec((1, HY_WIDTH), lambda j, i: (0, 0)),
                  pl.BlockSpec((1, HY_WIDTH), lambda j, i: (0, 0)),
                  pl.BlockSpec((HY_WIDTH, tn),
                               lambda j, i: (0, (j // nj) * 2 * nj + jnp.where(i < n_anti, nj, 0) + j % nj)),
                  pl.BlockSpec((1, tn), lambda j, i: (0, j % nj))],
        out_specs=[pl.BlockSpec((tm, tn), lambda j, i: (i, j)),
                   pl.BlockSpec((1, tn), lambda j, i: (0, j))],
        out_shape=[jax.ShapeDtypeStruct((2 * L, HY_ORDER * D), F32),
                   jax.ShapeDtypeStruct((1, HY_ORDER * D), F32)],
        compiler_params=_cparams(2),
    )(z, w1p, b1.reshape(1, -1), w2, b2.reshape(1, -1), freq.reshape(1, -1), w3, deltas.reshape(1, D))


def _bmm_kernel(*refs, n_pair, has_scale, has_gate):
    it = iter(refs)
    a_refs = [next(it) for _ in range(n_pair)]
    x_refs = [next(it) for _ in range(n_pair)]
    sc_ref = next(it) if has_scale else None
    if has_gate:
        g_ref, z_ref, zb_ref = next(it), next(it), next(it)
    o_ref = next(it)
    acc = None
    for a, x in zip(a_refs, x_refs):
        p = jnp.dot(a[...], x[...].astype(BF16), preferred_element_type=F32)
        acc = p if acc is None else acc + p
    if has_scale:
        acc = acc / sc_ref[...]
    if has_gate:
        acc = g_ref[...] * (acc + z_ref[...] * zb_ref[...])
    o_ref[...] = acc.astype(o_ref.dtype)


def _bmm(a_list, x_list, *, nblk, tr, tn, out_rows, out_cols, out_dtype, out_nblk=None, out_boff=0,
         colscale=None, gate=None, zin=None, zbias=None, o_prev=None):
    out_nblk = nblk if out_nblk is None else out_nblk
    in_specs, args = [], []
    for a, cb, kb in a_list:
        in_specs.append(pl.BlockSpec((tr, kb), lambda b, i, j, cb=cb: (i, cb)))
        args.append(a)
    for (x, boff, coff), (_, _, kb) in zip(x_list, a_list):
        in_specs.append(pl.BlockSpec((None, kb, tn), lambda b, i, j, boff=boff, coff=coff: (b + boff, 0, coff + j)))
        args.append(x)
    if colscale is not None:
        in_specs.append(pl.BlockSpec((1, tn), lambda b, i, j: (0, j)))
        args.append(colscale)
    if gate is not None:
        for arr, boff, coff in (gate, zin):
            in_specs.append(pl.BlockSpec((None, tr, tn),
                                         lambda b, i, j, boff=boff, coff=coff: (b + boff, i, coff + j)))
            args.append(arr)
        in_specs.append(pl.BlockSpec((1, tn), lambda b, i, j: (0, j)))
        args.append(zbias)
    aliases = {}
    n_real = len(args)
    if o_prev is not None:
        in_specs.append(pl.BlockSpec(memory_space=pl.ANY))
        args.append(o_prev)
        aliases = {n_real: 0}

    def kern(*refs):
        refs = list(refs)
        if o_prev is not None:
            refs.pop(n_real)
        _bmm_kernel(*refs, n_pair=len(a_list), has_scale=colscale is not None, has_gate=gate is not None)

    return pl.pallas_call(
        kern,
        grid=(nblk, out_rows // tr, out_cols // tn),
        in_specs=in_specs,
        out_specs=pl.BlockSpec((None, tr, tn), lambda b, i, j: (b + out_boff, i, j)),
        out_shape=jax.ShapeDtypeStruct((out_nblk, out_rows, out_cols), out_dtype),
        input_output_aliases=aliases,
        compiler_params=_cparams(3),
    )(*args)


def _hy_mac_kernel(ure_ref, uim_ref, gre_ref, gim_ref, yre_ref, yim_ref, *, nb):
    for i in range(nb):
        acc_re = acc_im = None
        for j in range(nb):
            dd = i - j + nb - 1
            ur, ui = ure_ref[j], uim_ref[j]
            gr, gi = gre_ref[dd], gim_ref[dd]
            re = gr * ur - gi * ui
            im = gr * ui + gi * ur
            acc_re = re if acc_re is None else acc_re + re
            acc_im = im if acc_im is None else acc_im + im
        yre_ref[i] = acc_re.astype(yre_ref.dtype)
        yim_ref[i] = acc_im.astype(yim_ref.dtype)


def _hy_mac(u, g, order, B, nb, Pp, tc):
    tr = 384 if Pp % 384 == 0 else (256 if Pp % 256 == 0 else Pp)
    nr = Pp // tr
    nd = 2 * nb - 1
    u4 = u.reshape(B, nb, 2 * Pp, D)
    ospec = pl.BlockSpec((None, nb, tr, tc), lambda i, j, b: (b, 0, i, j))
    oshape = jax.ShapeDtypeStruct((B, nb, Pp, D), BF16)
    yre, yim = pl.pallas_call(
        functools.partial(_hy_mac_kernel, nb=nb),
        grid=(nr, D // tc, B),
        in_specs=[pl.BlockSpec((None, nb, tr, tc), lambda i, j, b: (b, 0, i, j)),
                  pl.BlockSpec((None, nb, tr, tc), lambda i, j, b: (b, 0, i + nr, j)),
                  pl.BlockSpec((nd, tr, tc), lambda i, j, b: (0, i, order * (D // tc) + j)),
                  pl.BlockSpec((nd, tr, tc), lambda i, j, b: (0, i + nr, order * (D // tc) + j))],
        out_specs=[ospec, ospec],
        out_shape=[oshape, oshape],
        compiler_params=_cparams(3),
    )(u4, u4, g, g)
    return yre.reshape(B * nb, Pp, D), yim.reshape(B * nb, Pp, D)


def _hyena_mixer(lay, h, x, mgate, w_in, conv_w, conv_b, f_w1, f_b1, f_w2, f_b2, f_w3, f_freq, f_bias,
                 w_out, tm):
    M = _M(lay)
    proj = _linear(h, [w_in], tm=tm, tn=1024)
    cv = _hy_conv(lay, proj, conv_w, conv_b, min(tm, lay.L0, 512), 512)
    tn = 512
    nj = D // tn
    z_all = None
    for seg in (0, 1):
        B, L, row_off = (lay.B0, lay.L0, 0) if seg == 0 else (lay.B1, lay.L1, _R0(lay))
        P, nb, Pp = _hy_geom(L)
        assert row_off % P == 0
        boff = row_off // P
        fwd, inv = _dft_mats(P)
        fext, nrm = _hy_filters(L, f_w1, f_b1, f_w2, f_b2, f_w3, f_freq, min(L, 512), tn)
        fx = fext.reshape(2 * nb, P, HY_ORDER * D)
        g = _bmm([(fwd, 0, P), (fwd, 1, P)], [(fx, 1, 0), (fx, 0, 0)], nblk=2 * nb - 1, tr=Pp, tn=tn,
                 out_rows=2 * Pp, out_cols=HY_ORDER * D, out_dtype=F32, colscale=nrm)
        cvv = cv.reshape(M // P, P, 3 * D)
        zsrc = (cvv, boff, 2 * nj)
        for n in range(HY_ORDER):
            u = _bmm([(fwd, 0, P)], [zsrc], nblk=B * nb, tr=Pp, tn=tn, out_rows=2 * Pp, out_cols=D,
                     out_dtype=F32)
            yre, yim = _hy_mac(u, g, n, B, nb, Pp, 256)
            last = n == HY_ORDER - 1
            z = _bmm([(inv, 0, Pp), (inv, 1, Pp)], [(yre, 0, 0), (yim, 0, 0)], nblk=B * nb, tr=min(P, 512),
                     tn=tn, out_rows=P, out_cols=D, out_dtype=F32,
                     out_nblk=(M // P) if last else None, out_boff=boff if last else 0,
                     gate=(cvv, boff, n * nj), zin=zsrc, zbias=f_bias[n].reshape(1, D),
                     o_prev=None if (not last or z_all is None) else z_all.reshape(M // P, P, D))
            zsrc = (z, boff if last else 0, 0)
        z_all = z.reshape(M, D)
    return _linear(z_all, [w_out], tm=tm, tn=1024, res=x, modgate=mgate, lay=lay)


def _rw_shift_kernel(x_ref, xp_ref, xn_ref, mu_ref, o_ref, *, lay, tm):
    j = pl.program_id(1)
    is_p, pos0, _ = _seq_pos(lay, pl.program_id(0) * tm)
    x = x_ref[...]
    hp = xp_ref[...]
    hn = xn_ref[...]
    rows = lax.broadcasted_iota(jnp.int32, (tm, 1), 0)
    pos = pos0 + rows
    col = pos & (GRID_W - 1)
    x_m1 = jnp.where(rows == 0, hp[GRID_W - 1:GRID_W], pltpu.roll(x, 1, 0))
    x_p1 = jnp.where(rows == tm - 1, hn[0:1], pltpu.roll(x, tm - 1, 0))
    if tm > GRID_W:
        x_mw = jnp.concatenate([hp, x[:tm - GRID_W]], axis=0)
        x_pw = jnp.concatenate([x[GRID_W:], hn], axis=0)
    else:
        x_mw, x_pw = hp, hn
    ok_m1 = jnp.where(is_p, pos, col) != 0
    ok_p1 = jnp.where(is_p, pos - (lay.L0 - 1), col - (GRID_W - 1)) != 0
    s_m1 = jnp.where(ok_m1, x_m1, 0.0)
    s_p1 = jnp.where(ok_p1, x_p1, 0.0)
    s_mw = jnp.where(pos >= GRID_W, x_mw, 0.0)
    s_pw = jnp.where(pos < lay.L1 - GRID_W, x_pw, 0.0)
    n_m1 = jnp.where(is_p, 2, 1)
    n_1 = jnp.where(is_p, 4, 2)
    sh = jnp.where(j < n_m1, s_m1, jnp.where(j < n_1, s_p1, jnp.where(j == 2, s_mw, s_pw)))
    xx = sh - x
    mu = mu_ref[...]
    for k in range(6):
        o_ref[k] = (x + xx * mu[k:k + 1]).astype(o_ref.dtype)


def _rw_shift(lay, h, mu, tm):
    M = h.shape[0]
    tc = D // 4
    g = GRID_W
    return pl.pallas_call(
        functools.partial(_rw_shift_kernel, lay=lay, tm=tm),
        grid=(M // tm, 4),
        in_specs=[pl.BlockSpec((tm, tc), lambda i, j: (i, j)),
                  pl.BlockSpec((g, tc), lambda i, j: (jnp.maximum(i * (tm // g) - 1, 0), j)),
                  pl.BlockSpec((g, tc), lambda i, j: (jnp.minimum((i + 1) * (tm // g), M // g - 1), j)),
                  pl.BlockSpec((6, tc), lambda i, j: (0, j))],
        out_specs=pl.BlockSpec((6, tm, tc), lambda i, j: (0, i, j)),
        out_shape=jax.ShapeDtypeStruct((6, M, D), BF16),
        compiler_params=_cparams(2),
    )(h, h, h, mu)


def _head_sum(x, seg):
    hi = x.astype(BF16)
    lo = (x - hi.astype(F32)).astype(BF16)
    return (jnp.dot(hi, seg, preferred_element_type=F32) + jnp.dot(lo, seg, preferred_element_type=F32))


def _head_seg(tc):
    r = lax.broadcasted_iota(jnp.int32, (tc, tc), 0) // RW_HEAD
    c = lax.broadcasted_iota(jnp.int32, (tc, tc), 1) // RW_HEAD
    return (r == c).astype(BF16)


def _rw_prep_kernel(r_ref, k_ref, v_ref, lw0_ref, lw1_ref, ar0_ref, ar1_ref, w0_ref, a0_ref, kk_p_ref,
                    ka_ref, rk_ref, kk_ref, w0o_ref, w1o_ref, kd0_ref, kd1_ref, b0_ref, b1_ref, bon_ref, *, tc):
    seg = _head_seg(tc)
    r, k, v = r_ref[...], k_ref[...], v_ref[...]
    kk = k * kk_p_ref[...]
    kk = kk * lax.rsqrt(_head_sum(kk * kk, seg) + 1e-12)
    kk_ref[...] = kk
    bonus = None
    for d, (lw_ref, ar_ref, wo_ref, kdo_ref, bo_ref) in enumerate(
            ((lw0_ref, ar0_ref, w0o_ref, kd0_ref, b0_ref), (lw1_ref, ar1_ref, w1o_ref, kd1_ref, b1_ref))):
        wo_ref[...] = -jnp.exp(-_softplus(-(w0_ref[d:d + 1, :] + lw_ref[...])) - 0.5)
        a = _sigmoid(a0_ref[d:d + 1, :] + ar_ref[...])
        kd = k * (1.0 + (a - 1.0) * ka_ref[...])
        kdo_ref[...] = kd
        bo_ref[...] = a * kk
        bon = _head_sum(r * kd * rk_ref[...], seg) * v
        bonus = bon if bonus is None else bonus + bon
    bon_ref[...] = bonus


def _rw_prep(r, k, v, lw0, lw1, ar0, ar1, w0, a0, k_k, k_a, r_k, tm, tc):
    M = r.shape[0]
    big = pl.BlockSpec((tm, tc), lambda i, j: (i, j))
    two = pl.BlockSpec((2, tc), lambda i, j: (0, j))
    one = pl.BlockSpec((1, tc), lambda i, j: (0, j))
    shape = jax.ShapeDtypeStruct((M, D), F32)
    return pl.pallas_call(
        functools.partial(_rw_prep_kernel, tc=tc),
        grid=(M // tm, D // tc),
        in_specs=[big] * 7 + [two, two, one, one, one],
        out_specs=[big] * 8,
        out_shape=[shape] * 8,
        compiler_params=_cparams(2),
    )(r, k, v, lw0, lw1, ar0, ar1, w0, a0, k_k.reshape(1, D), k_a.reshape(1, D), r_k.reshape(1, D))


def _rw_scan_kernel(*refs, reverse, T, Hb, zero_init, emit_state, nT):
    it = iter(refs)
    r_ref, w_ref, k_ref, v_ref, kk_ref, b_ref = (next(it) for _ in range(6))
    s0_ref = None if zero_init else next(it)
    y_ref = next(it)
    so_ref = next(it) if emit_state else None
    s_scr = next(it)
    C, Kd = CHUNK, RW_HEAD
    P2 = 2 * Kd
    npair = Hb // 2
    t = pl.program_id(2)
    zero_blk = jnp.zeros((Kd, Kd), F32)

    @pl.when(t == 0)
    def _():
        for p in range(npair):
            if zero_init:
                s_scr[p] = jnp.zeros((P2, P2), F32)
            else:
                s_scr[p] = jnp.concatenate(
                    [jnp.concatenate([s0_ref[2 * p], zero_blk], axis=1),
                     jnp.concatenate([zero_blk, s0_ref[2 * p + 1]], axis=1)], axis=0)

    row = lax.broadcasted_iota(jnp.int32, (C, C), 0)
    col = lax.broadcasted_iota(jnp.int32, (C, C), 1)
    tri = ((row <= col) if reverse else (row >= col)).astype(F32)
    row2 = lax.broadcasted_iota(jnp.int32, (P2, P2), 0)
    col2 = lax.broadcasted_iota(jnp.int32, (P2, P2), 1)
    same = (row2 // C) == (col2 // C)
    tr, tc_ = row2 % C, col2 % C
    incl2 = same & ((tr <= tc_) if reverse else (tr >= tc_))
    strict2 = same & ((tr < tc_) if reverse else (tr > tc_))
    eye2 = (row2 == col2).astype(F32)
    lane = lax.broadcasted_iota(jnp.int32, (C, P2), 1)
    h0 = lane < Kd
    last_i = 0 if reverse else C - 1
    nc = T // C
    nt_dims = (((1,), (1,)), ((), ()))
    tn_dims = (((0,), (0,)), ((), ()))

    def mm(a, b):
        return jnp.dot(a.astype(BF16), b.astype(BF16), preferred_element_type=F32)

    def stack(x):
        return jnp.concatenate([jnp.where(h0, x, 0.0), jnp.where(h0, 0.0, x)], axis=0)

    def fold(x2):
        return x2[:C] + x2[C:]

    states = [s_scr[p] for p in range(npair)]
    order = [(nc - 1 - i) if reverse else i for i in range(nc)]
    items = [(cc, p) for cc in order for p in range(npair)]
    wide = {}
    for cc in order:
        rows = slice(cc * C, (cc + 1) * C)
        lw = w_ref[rows, :]
        cum = jnp.dot(tri, lw, precision=HIGHEST, preferred_element_type=F32)
        tot = cum[last_i:last_i + 1, :]
        e_neg = jnp.exp(-cum)
        e_rem = jnp.exp(tot - cum)
        kd_, b_ = k_ref[rows, :], b_ref[rows, :]
        wide[cc] = dict(gam=jnp.exp(tot), alpha=kk_ref[rows, :] * jnp.exp(cum - lw),
                        rho=r_ref[rows, :] * jnp.exp(cum), beta=b_ * e_neg, kappa=kd_ * e_neg,
                        bet2=b_ * e_rem, kap2=kd_ * e_rem, v=v_ref[rows, :])

    def pair(cc, p, name):
        return wide[cc][name][:, p * P2:(p + 1) * P2]

    a2 = [stack(pair(cc, p, "alpha")) for cc, p in items]
    r2 = [stack(pair(cc, p, "rho")) for cc, p in items]
    v2 = [stack(pair(cc, p, "v")) for cc, p in items]
    big = [lax.dot_general(
        jnp.concatenate([a, r], axis=0).astype(BF16),
        jnp.concatenate([stack(pair(cc, p, "beta")), stack(pair(cc, p, "kappa"))], axis=0).astype(BF16),
        nt_dims, preferred_element_type=F32) for a, r, (cc, p) in zip(a2, r2, items)]
    xk = [jnp.where(strict2, -g[:P2, :P2], 0.0) for g in big]
    l_ak = [jnp.where(strict2, g[:P2, P2:], 0.0) for g in big]
    m_rb = [jnp.where(incl2, g[P2:, :P2], 0.0) for g in big]
    m_rk = [jnp.where(incl2, g[P2:, P2:], 0.0) for g in big]
    tinv = [eye2 + x for x in xk]
    for _ in range(5):
        xk = [mm(x, x) for x in xk]
        tinv = [tv + mm(tv, x) for tv, x in zip(tinv, xk)]
    lv2 = [mm(l, v) for l, v in zip(l_ak, v2)]
    au = [mm(tv, jnp.concatenate([a, lv], axis=1)) for tv, a, lv in zip(tinv, a2, lv2)]
    mau = [mm(m, x) for m, x in zip(m_rb, au)]
    mv = [mm(m, v) for m, v in zip(m_rk, v2)]
    r_t = [fold(r - ma[:, :P2]).astype(BF16) for r, ma in zip(r2, mau)]
    y0 = [fold(m - ma[:, P2:]) for m, ma in zip(mv, mau)]
    g_low, h_add = [], []
    for x, (cc, p) in zip(au, items):
        a_t, u_t = fold(x[:, :P2]), fold(x[:, P2:])
        bet2 = pair(cc, p, "bet2")
        g_low.append(jnp.where(same, lax.dot_general(a_t.astype(BF16), bet2.astype(BF16), tn_dims,
                                                     preferred_element_type=F32), 0.0).astype(BF16))
        vu = jnp.concatenate([pair(cc, p, "v"), u_t], axis=0).astype(BF16)
        kb = jnp.concatenate([pair(cc, p, "kap2"), -bet2], axis=0).astype(BF16)
        h_add.append(jnp.where(same, lax.dot_general(vu, kb, tn_dims, preferred_element_type=F32), 0.0))
    y_rows = [None] * nc
    for n, cc in enumerate(order):
        idx = [n * npair + p for p in range(npair)]
        sb = [s.astype(BF16) for s in states]
        y_pairs = [y0[i] + lax.dot_general(r_t[i], sb[p], nt_dims, preferred_element_type=F32)
                   for p, i in enumerate(idx)]
        states = [states[p] * pair(cc, p, "gam") - jnp.dot(sb[p], g_low[i], preferred_element_type=F32)
                  + h_add[i] for p, i in enumerate(idx)]
        y_rows[cc] = jnp.concatenate(y_pairs, axis=1) if npair > 1 else y_pairs[0]
    y_ref[...] = jnp.concatenate(y_rows, axis=0) if nc > 1 else y_rows[0]
    for p in range(npair):
        s_scr[p] = states[p]

    if emit_state:
        @pl.when(t == nT - 1)
        def _():
            for p in range(npair):
                so_ref[2 * p] = states[p][:Kd, :Kd]
                so_ref[2 * p + 1] = states[p][Kd:, Kd:]


def _rw_scan(lay, seg, direction, r, lw, kd, v, kk, b, s0, y_prev):
    M = _M(lay)
    Hb = RW_SCAN_HEADS
    H = RW_HEADS
    if seg == 0:
        B, L, boff = lay.B0, lay.L0, 0
    else:
        B, L, boff = lay.B1, lay.L1, _R0(lay) // lay.L1
    reverse = direction == 1
    T = min(L, RW_SCAN_ROWS)
    nT = L // T
    zero_init = s0 is None
    emit_state = seg == 0
    wd = Hb * RW_HEAD

    def tt(t):
        return (nT - 1 - t) if reverse else t

    spec = pl.BlockSpec((None, T, wd), lambda bb, h, t: (bb + boff, tt(t), h))
    args = [a.reshape(M // L, L, D) for a in (r, lw, kd, v, kk, b)]
    in_specs = [spec] * 6
    if not zero_init:
        in_specs.append(pl.BlockSpec((None, None, Hb, RW_HEAD, RW_HEAD),
                                     lambda bb, h, t: (bb, direction, h, 0, 0)))
        args.append(s0)
    aliases = {}
    n_real = len(args)
    if y_prev is not None:
        in_specs.append(pl.BlockSpec(memory_space=pl.ANY))
        args.append(y_prev.reshape(M // L, L, D))
        aliases = {n_real: 0}
    out_specs = [spec]
    out_shape = [jax.ShapeDtypeStruct((M // L, L, D), F32)]
    if emit_state:
        out_specs.append(pl.BlockSpec((None, Hb, RW_HEAD, RW_HEAD), lambda bb, h, t: (bb, h, 0, 0)))
        out_shape.append(jax.ShapeDtypeStruct((B, H, RW_HEAD, RW_HEAD), F32))

    def kern(*refs):
        refs = list(refs)
        if y_prev is not None:
            refs.pop(n_real)
        _rw_scan_kernel(*refs, reverse=reverse, T=T, Hb=Hb, zero_init=zero_init, emit_state=emit_state, nT=nT)

    outs = pl.pallas_call(
        kern,
        grid=(B, H // Hb, nT),
        in_specs=in_specs,
        out_specs=out_specs,
        out_shape=out_shape,
        scratch_shapes=[pltpu.VMEM((Hb // 2, 2 * RW_HEAD, 2 * RW_HEAD), F32)],
        input_output_aliases=aliases,
        compiler_params=_cparams(3),
    )(*args)
    return outs[0].reshape(M, D), (outs[1] if emit_state else None)


def _rw_post_kernel(y0_ref, y1_ref, bon_ref, g_ref, lnw_ref, lnb_ref, o_ref, *, tc):
    seg = _head_seg(tc)
    y = y0_ref[...] + y1_ref[...]
    mean = _head_sum(y, seg) * (1.0 / RW_HEAD)
    yc = y - mean
    var = _head_sum(yc * yc, seg) * (1.0 / RW_HEAD)
    y = yc * lax.rsqrt(var + RW_GN_EPS) * lnw_ref[...] + lnb_ref[...]
    o_ref[...] = ((y + bon_ref[...]) * g_ref[...]).astype(o_ref.dtype)


def _rw_post(y0, y1, bonus, gate, ln_w, ln_b, tm, tc):
    M = y0.shape[0]
    big = pl.BlockSpec((tm, tc), lambda i, j: (i, j))
    one = pl.BlockSpec((1, tc), lambda i, j: (0, j))
    return pl.pallas_call(
        functools.partial(_rw_post_kernel, tc=tc),
        grid=(M // tm, D // tc),
        in_specs=[big] * 4 + [one, one],
        out_specs=big,
        out_shape=jax.ShapeDtypeStruct((M, D), BF16),
        compiler_params=_cparams(2),
    )(y0, y1, bonus, gate, ln_w.reshape(1, D), ln_b.reshape(1, D))


def _rwkv7_mixer(lay, h, x, mgate, s0, mu, w_r, w_k, w_v, w_o, w0, w1, w2, a0, a1, a2, g1, g2,
                 k_k, k_a, r_k, ln_w, ln_b, tm):
    x6 = _rw_shift(lay, h, mu, min(lay.L0, 256))
    r = _linear(x6, [w_r], x_idx=0, tm=tm, tn=1024)
    k = _linear(x6, [w_k], x_idx=2, tm=tm, tn=1024)
    v = _linear(x6, [w_v], x_idx=3, tm=tm, tn=1024)
    gh = _linear(x6, [g1], x_idx=5, tm=tm, tn=256, act="sigmoid", out_dtype=BF16)
    gate = _linear(gh, [g2], tm=tm, tn=1024)
    rk = RW_RANK

    def cat_in(w):
        out = jnp.zeros((D, 2 * LANES), F32)
        return out.at[:, :rk].set(w[0]).at[:, LANES:LANES + rk].set(w[1])

    def pad_out(w, d):
        return jnp.zeros((2 * LANES, D), F32).at[d * LANES:d * LANES + rk].set(w)

    tw = _linear(x6, [cat_in(w1)], x_idx=1, tm=tm, tn=2 * LANES, act="tanh", out_dtype=BF16)
    ta = _linear(x6, [cat_in(a1)], x_idx=4, tm=tm, tn=2 * LANES)
    lws = [_linear(tw, [pad_out(w2[d], d)], tm=tm, tn=1024) for d in range(2)]
    ars = [_linear(ta, [pad_out(a2[d], d)], tm=tm, tn=1024) for d in range(2)]
    kk, lw0, lw1, kd0, kd1, b0, b1, bonus = _rw_prep(r, k, v, lws[0], lws[1], ars[0], ars[1], w0, a0,
                                                     k_k, k_a, r_k, min(tm, 512), 256)
    ys, states = [], []
    for d, (lw, kd, b) in enumerate(((lw0, kd0, b0), (lw1, kd1, b1))):
        y, st = _rw_scan(lay, 0, d, r, lw, kd, v, kk, b, None, None)
        y, _ = _rw_scan(lay, 1, d, r, lw, kd, v, kk, b, s0, y)
        ys.append(y)
        states.append(st)
    yo = _rw_post(ys[0], ys[1], bonus, gate, ln_w, ln_b, min(tm, 512), 256)
    x = _linear(yo, [w_o], tm=tm, tn=1024, res=x, modgate=mgate, lay=lay)
    return x, jnp.stack(states, axis=1)


def _router_kernel(x_ref, w_ref, o_ref, *, n_exp):
    logits = jnp.dot(x_ref[...].astype(F32), w_ref[...], precision=HIGHEST, preferred_element_type=F32)
    lane = lax.broadcasted_iota(jnp.int32, logits.shape, 1)
    neg = -jnp.inf
    lg = jnp.where(lane < n_exp, logits, neg)
    m1 = jnp.max(lg, axis=-1, keepdims=True)
    i1 = jnp.min(jnp.where(lg == m1, lane, LANES), axis=-1, keepdims=True)
    lg2 = jnp.where(lane == i1, neg, lg)
    m2 = jnp.max(lg2, axis=-1, keepdims=True)
    i2 = jnp.min(jnp.where(lg2 == m2, lane, LANES), axis=-1, keepdims=True)
    e = jnp.exp(m2 - m1)
    p1 = 1.0 / (1.0 + e)
    o_ref[...] = jnp.where(lane == i1, p1, 0.0) + jnp.where(lane == i2, e * p1, 0.0)


def _router(h, router, tm):
    M = h.shape[0]
    n_exp = router.shape[1]
    wp = jnp.zeros((D, LANES), F32).at[:, :n_exp].set(router)
    return pl.pallas_call(
        functools.partial(_router_kernel, n_exp=n_exp),
        grid=(M // tm,),
        in_specs=[pl.BlockSpec((tm, D), lambda i: (i, 0)), pl.BlockSpec((D, LANES), lambda i: (0, 0))],
        out_specs=pl.BlockSpec((tm, LANES), lambda i: (i, 0)),
        out_shape=jax.ShapeDtypeStruct((M, LANES), F32),
        compiler_params=_cparams(1),
    )(h, wp)


def _moe(lay, h, x, mgate, router, w1, w3, w2, e0, n_exp, tm):
    gates = _router(h, router, min(tm, 512))
    for e in range(n_exp):
        a = _linear(h, [w1, w3], w_idx=e0 + e, tm=tm, tn=256, swiglu=True, out_dtype=BF16)
        x = _linear(a, [w2], w_idx=e0 + e, tm=tm, tn=1024, res=x, modgate=mgate, lay=lay,
                    rowgate=gates, rowgate_col=e)
    return x


def kernel(x_prompt, x_sample, state_l0_gla, state_l2_hgrn2, state_l3_rwkv7, c, c_ctx, norm_mix_g, norm_ffn_g, mod_w, mod_b, gla_w_in, gla_gk_w1, gla_gk_w2, gla_gk_b, gla_norm_g, gla_w_out, hy_w_in, hy_conv_w, hy_conv_b, hy_f_w1, hy_f_b1, hy_f_w2, hy_f_b2, hy_f_w3, hy_f_freq, hy_f_bias, hy_w_out, hg_w_in, hg_lb, hg_norm_g, hg_w_out, rw_mu, rw_w_r, rw_w_k, rw_w_v, rw_w_o, rw_w0, rw_w1, rw_w2, rw_a0, rw_a1, rw_a2, rw_g1, rw_g2, rw_k_k, rw_k_a, rw_r_k, rw_ln_w, rw_ln_b, ffn_w1, ffn_w3, ffn_w2, moe_router, moe_w1, moe_w3, moe_w2, final_norm_g):
    lay = Lay(x_prompt.shape[0], x_prompt.shape[1], x_sample.shape[0], x_sample.shape[1])
    M = _M(lay)
    r0 = _R0(lay)
    tm = _row_tile(lay, 1024)
    tm_norm = _row_tile(lay, 512)
    x = jnp.concatenate([x_prompt.reshape(-1, D), x_sample.reshape(-1, D)], axis=0)
    cond = jnp.zeros((8, D), F32).at[0].set(c_ctx).at[1:1 + lay.B1].set(c)
    n_exp = moe_w1.shape[1]
    moe_w1r = moe_w1.reshape((-1,) + moe_w1.shape[2:])
    moe_w3r = moe_w3.reshape((-1,) + moe_w3.shape[2:])
    moe_w2r = moe_w2.reshape((-1,) + moe_w2.shape[2:])
    new_states = {}
    for l in range(DEPTH):
        m = _linear(cond, [mod_w], w_idx=l, tm=8, tn=1024, bias=mod_b[l], pre_silu=True)
        mods = [m[:, k * D:(k + 1) * D].reshape(8, 1, D) for k in range(6)]
        kind = l % 4
        h = _norm(x, norm_mix_g[l], lay=lay, tm=tm_norm, out_dtype=F32 if kind == 3 else BF16,
                  shift=mods[0], scale=mods[1])
        if kind == 0:
            x, new_states[0] = _gla_mixer(lay, h, x, mods[2], state_l0_gla, gla_w_in, gla_gk_w1, gla_gk_w2,
                                          gla_gk_b, gla_norm_g, gla_w_out, tm)
        elif kind == 1:
            x = _hyena_mixer(lay, h, x, mods[2], hy_w_in, hy_conv_w, hy_conv_b, hy_f_w1, hy_f_b1, hy_f_w2,
                             hy_f_b2, hy_f_w3, hy_f_freq, hy_f_bias, hy_w_out, tm)
        elif kind == 2:
            x, new_states[2] = _hgrn2_mixer(lay, h, x, mods[2], state_l2_hgrn2, l, hg_w_in, hg_lb, hg_norm_g,
                                            hg_w_out, tm)
        else:
            x, new_states[3] = _rwkv7_mixer(lay, h, x, mods[2], state_l3_rwkv7, rw_mu, rw_w_r, rw_w_k, rw_w_v,
                                            rw_w_o, rw_w0, rw_w1, rw_w2, rw_a0, rw_a1, rw_a2, rw_g1, rw_g2,
                                            rw_k_k, rw_k_a, rw_r_k, rw_ln_w, rw_ln_b, tm)
        h = _norm(x, norm_ffn_g[l], lay=lay, tm=tm_norm, out_dtype=BF16, shift=mods[3], scale=mods[4])
        j = l // 2
        if l % 2 == 0:
            a = _linear(h, [ffn_w1, ffn_w3], w_idx=j, tm=tm, tn=512, swiglu=True, out_dtype=BF16)
            x = _linear(a, [ffn_w2], w_idx=j, tm=tm, tn=512, res=x, modgate=mods[5], lay=lay)
        else:
            x = _moe(lay, h, x, mods[5], moe_router[j], moe_w1r, moe_w3r, moe_w2r, j * n_exp, n_exp, tm)
    yp = _norm(x, final_norm_g, lay=lay, tm=tm_norm, out_dtype=F32, row0=0, nrows=r0)
    ys = _norm(x, final_norm_g, lay=lay, tm=tm_norm, out_dtype=F32, row0=r0, nrows=M - r0)
    return (yp.reshape(x_prompt.shape), ys.reshape(x_sample.shape), new_states[0], new_states[2],
            new_states[3])
```

```python
import collections
import functools
import math

import jax
import jax.numpy as jnp
from jax import lax
from jax.experimental import pallas as pl
from jax.experimental.pallas import tpu as pltpu

F32 = jnp.float32
BF16 = jnp.bfloat16
HIGHEST = lax.Precision.HIGHEST

D = 2048
DEPTH = 4
CHUNK = 64
EPS = 1e-6
GRID_W = 64
GLA_HEADS, GLA_DK, GLA_DV = 4, 256, 512
GLA_GATE_RANK, GLA_GATE_NORM = 16, 16.0
HY_ORDER, HY_BANDS, HY_WIDTH = 2, 16, 64
HY_FAST, HY_SLOW, HY_TARGET = 0.3, 1.5, 1e-2
HG_HEADS, HG_DK, HG_DV = 16, 128, 128
RW_HEADS, RW_HEAD = 32, 64
RW_RANK = 96
RW_GN_EPS = 64e-5
N_EXPERTS = 8
LANES = 128
VMEM_LIMIT_MB = 56
LINEAR_VMEM_BUDGET = 44 * 1024 * 1024
HY_BLOCK_MAX = 1024
RW_SCAN_HEADS = 8
RW_SCAN_ROWS = 256
GLA_SCAN_ROWS = 512
MOE_BLOCK = 1024
MOE_SUB = 128
MOE_FFN_ROWS = 512

Lay = collections.namedtuple("Lay", "B0 L0 B1 L1")


def _R0(lay):
    return lay.B0 * lay.L0


def _M(lay):
    return lay.B0 * lay.L0 + lay.B1 * lay.L1


def _group_of_row(lay, r):
    r0 = _R0(lay)
    return jnp.where(r < r0, 0, 1 + (r - r0) // lay.L1)


def _cparams(n_axes, vmem_mb=VMEM_LIMIT_MB):
    return pltpu.CompilerParams(dimension_semantics=("arbitrary",) * n_axes,
                                vmem_limit_bytes=vmem_mb * 1024 * 1024)


def _row_tile(lay, cap):
    t = cap
    while (_R0(lay) % t) or (lay.L1 % t) or (lay.L0 % t and t % lay.L0):
        t //= 2
    return t


def _sigmoid(x):
    return 1.0 / (1.0 + jnp.exp(-x))


def _softplus(x):
    return jnp.maximum(x, 0.0) + jnp.log(1.0 + jnp.exp(-jnp.abs(x)))


def _linear_kernel(*refs, n_w, cache_w, has_bias, has_res, has_mod, has_row, row_col, act,
                   pre_silu, swiglu, hp):
    it = iter(refs)
    x_ref = next(it)
    w_refs = [next(it) for _ in range(n_w)]
    b_ref = next(it) if has_bias else None
    res_ref = next(it) if has_res else None
    mod_ref = next(it) if has_mod else None
    row_ref = next(it) if has_row else None
    o_ref = next(it)
    scr = [next(it) for _ in range(n_w)] if cache_w else None

    x = x_ref[...]
    if pre_silu:
        x = x.astype(F32)
        x = x * _sigmoid(x)
    if hp:
        accs = [jnp.dot(x.astype(F32), w[...].astype(F32), precision=HIGHEST,
                        preferred_element_type=F32) for w in w_refs]
    else:
        if cache_w:
            @pl.when(pl.program_id(1) == 0)
            def _():
                for s, w in zip(scr, w_refs):
                    s[...] = w[...].astype(BF16)
            wv = [s[...] for s in scr]
        else:
            wv = [w[...].astype(BF16) for w in w_refs]
        xb = x.astype(BF16)
        accs = [jnp.dot(xb, w, preferred_element_type=F32) for w in wv]
    if swiglu:
        a = accs[0]
        acc = a * _sigmoid(a) * accs[1]
    else:
        acc = accs[0]
    if has_bias:
        acc = acc + b_ref[...]
    if act == "tanh":
        acc = jnp.tanh(acc)
    elif act == "sigmoid":
        acc = _sigmoid(acc)
    if has_row:
        acc = acc * row_ref[:, row_col:row_col + 1]
    if has_mod:
        acc = acc * mod_ref[...]
    if has_res:
        acc = res_ref[...] + acc
    o_ref[...] = acc.astype(o_ref.dtype)


def _linear(x, ws, *, tm, tn, out_dtype=F32, w_idx=None, x_idx=None, bias=None, act=None,
            pre_silu=False, swiglu=False, res=None, modgate=None, lay=None, rowgate=None,
            rowgate_col=0, hp=False):
    n_w = len(ws)
    M, K = x.shape[-2:]
    N = ws[0].shape[-1]
    tm = min(tm, M)
    tn = min(tn, N)

    def vmem_bytes(tm_):
        cache = (not hp) and M // tm_ > 1 and ws[0].dtype != BF16
        b = 2 * tm_ * K * x.dtype.itemsize
        b += n_w * K * tn * (2 * ws[0].dtype.itemsize + (2 if cache else 0))
        b += 2 * tm_ * tn * jnp.dtype(out_dtype).itemsize + 3 * tm_ * tn * 4
        b += 2 * tm_ * tn * 4 if res is not None else 0
        b += 2 * tm_ * LANES * 4 if rowgate is not None else 0
        return b

    while vmem_bytes(tm) > LINEAR_VMEM_BUDGET and tm > 256:
        tm //= 2
    assert M % tm == 0 and N % tn == 0, (M, tm, N, tn)
    n_i = M // tm
    cache_w = (not hp) and n_i > 1 and ws[0].dtype != BF16

    if x_idx is None:
        x_spec = pl.BlockSpec((tm, K), lambda j, i: (i, 0))
    else:
        x_spec = pl.BlockSpec((None, tm, K), lambda j, i: (x_idx, i, 0))
    if w_idx is None:
        w_spec = pl.BlockSpec((K, tn), lambda j, i: (0, j))
    else:
        w_spec = pl.BlockSpec((None, K, tn), lambda j, i: (w_idx, 0, j))
    in_specs = [x_spec] + [w_spec] * n_w
    args = [x] + list(ws)
    if bias is not None:
        in_specs.append(pl.BlockSpec((1, tn), lambda j, i: (0, j)))
        args.append(bias.reshape(1, N).astype(F32))
    if res is not None:
        in_specs.append(pl.BlockSpec((tm, tn), lambda j, i: (i, j)))
        args.append(res)
    if modgate is not None:
        in_specs.append(pl.BlockSpec((None, 1, tn), lambda j, i: (_group_of_row(lay, i * tm), 0, j)))
        args.append(modgate)
    if rowgate is not None:
        in_specs.append(pl.BlockSpec((tm, LANES), lambda j, i: (i, 0)))
        args.append(rowgate)
    kern = functools.partial(
        _linear_kernel, n_w=n_w, cache_w=cache_w, has_bias=bias is not None, has_res=res is not None,
        has_mod=modgate is not None, has_row=rowgate is not None, row_col=rowgate_col, act=act,
        pre_silu=pre_silu, swiglu=swiglu, hp=hp)
    scratch = [pltpu.VMEM((K, tn), BF16) for _ in range(n_w)] if cache_w else []
    return pl.pallas_call(
        kern,
        grid=(N // tn, n_i),
        in_specs=in_specs,
        out_specs=pl.BlockSpec((tm, tn), lambda j, i: (i, j)),
        out_shape=jax.ShapeDtypeStruct((M, N), out_dtype),
        scratch_shapes=scratch,
        compiler_params=_cparams(2),
    )(*args)


def _norm_kernel(*refs, has_mod):
    if has_mod:
        x_ref, g_ref, sh_ref, sc_ref, o_ref = refs
    else:
        x_ref, g_ref, o_ref = refs
    x = x_ref[...]
    y = x * lax.rsqrt(jnp.mean(x * x, axis=-1, keepdims=True) + EPS) * g_ref[...]
    if has_mod:
        y = y * (1.0 + sc_ref[...]) + sh_ref[...]
    o_ref[...] = y.astype(o_ref.dtype)


def _norm(x, g, *, lay, tm, out_dtype, shift=None, scale=None, row0=0, nrows=None):
    nrows = x.shape[0] if nrows is None else nrows
    assert row0 % tm == 0 and nrows % tm == 0
    off = row0 // tm
    has_mod = shift is not None
    in_specs = [pl.BlockSpec((tm, D), lambda i: (i + off, 0)),
                pl.BlockSpec((1, D), lambda i: (0, 0))]
    args = [x, g.reshape(1, D)]
    if has_mod:
        mspec = pl.BlockSpec((None, 1, D), lambda i: (_group_of_row(lay, (i + off) * tm), 0, 0))
        in_specs += [mspec, mspec]
        args += [shift, scale]
    return pl.pallas_call(
        functools.partial(_norm_kernel, has_mod=has_mod),
        grid=(nrows // tm,),
        in_specs=in_specs,
        out_specs=pl.BlockSpec((tm, D), lambda i: (i, 0)),
        out_shape=jax.ShapeDtypeStruct((nrows, D), out_dtype),
        compiler_params=_cparams(1),
    )(*args)


def _scan_kernel(*refs, mode, reverse, T, Hb, dk, dv, zero_init, emit_state, nT):
    it = iter(refs)
    q_ref = next(it)
    k_ref = next(it) if mode == "gla" else None
    f_ref = next(it)
    v_ref = next(it)
    lb_ref = next(it) if mode == "hgrn" else None
    s0_ref = None if zero_init else next(it)
    o_ref = next(it)
    so_ref = next(it) if emit_state else None
    s_scr = next(it)
    C = CHUNK
    t = pl.program_id(2)

    @pl.when(t == 0)
    def _():
        for hh in range(Hb):
            if zero_init:
                s_scr[hh] = jnp.zeros((dv, dk), F32)
            else:
                s_scr[hh] = s0_ref[hh].T

    row = lax.broadcasted_iota(jnp.int32, (C, C), 0)
    col = lax.broadcasted_iota(jnp.int32, (C, C), 1)
    incl = (row <= col) if reverse else (row >= col)
    tri = incl.astype(F32)
    ref_i = (C - 1 - C // 2) if reverse else C // 2
    last_i = 0 if reverse else C - 1
    nc = T // C
    nt_dims = (((1,), (1,)), ((), ()))
    tn_dims = (((0,), (0,)), ((), ()))

    order = [(nc - 1 - i) if reverse else i for i in range(nc)]
    items = [(cc, hh) for cc in order for hh in range(Hb)]
    wide = {}
    for cc in order:
        rows = slice(cc * C, (cc + 1) * C)
        q = q_ref[rows, :]
        fx = f_ref[rows, :]
        if mode == "gla":
            k = k_ref[rows, :]
            logf = (jnp.minimum(fx, 0.0) - jnp.log(1.0 + jnp.exp(-jnp.abs(fx)))) * (1.0 / GLA_GATE_NORM)
            q = q * dk ** -0.5
        else:
            lb = lb_ref[...]
            f = lb + (1.0 - lb) * _sigmoid(fx)
            k = 1.0 - f
            logf = jnp.log(f)
            q = q * _sigmoid(q) * dk ** -0.5
        b = jnp.dot(tri, logf, precision=HIGHEST, preferred_element_type=F32)
        b_mid = b[ref_i:ref_i + 1, :]
        b_last = b[last_i:last_i + 1, :]
        wide[cc] = dict(qe=(q * jnp.exp(b - b_mid)).astype(BF16), ke=(k * jnp.exp(b_mid - b)).astype(BF16),
                        qi=(q * jnp.exp(b)).astype(BF16), ku=(k * jnp.exp(b_last - b)).astype(BF16),
                        dec=jnp.exp(b_last), v=v_ref[rows, :].astype(BF16))

    def head(cc, hh, name):
        w = dv if name == "v" else dk
        return wide[cc][name][:, hh * w:(hh + 1) * w]

    sc = [lax.dot_general(head(cc, hh, "qe"), head(cc, hh, "ke"), nt_dims, preferred_element_type=F32)
          for cc, hh in items]
    sc = [jnp.where(incl, s, 0.0).astype(BF16) for s in sc]
    o_intra = [jnp.dot(s, head(cc, hh, "v"), preferred_element_type=F32) for s, (cc, hh) in zip(sc, items)]
    upd = [lax.dot_general(head(cc, hh, "v"), head(cc, hh, "ku"), tn_dims, preferred_element_type=F32)
           for cc, hh in items]
    states = [s_scr[hh] for hh in range(Hb)]
    o_rows = [None] * nc
    for n, cc in enumerate(order):
        idx = [n * Hb + hh for hh in range(Hb)]
        o_heads = [o_intra[i] + lax.dot_general(head(cc, hh, "qi"), states[hh].astype(BF16), nt_dims,
                                                preferred_element_type=F32) for hh, i in enumerate(idx)]
        states = [states[hh] * head(cc, hh, "dec") + upd[i] for hh, i in enumerate(idx)]
        o_rows[cc] = jnp.concatenate(o_heads, axis=1) if Hb > 1 else o_heads[0]
    o_ref[...] = jnp.concatenate(o_rows, axis=0) if nc > 1 else o_rows[0]
    for hh in range(Hb):
        s_scr[hh] = states[hh]

    if emit_state:
        @pl.when(t == nT - 1)
        def _():
            for hh in range(Hb):
                so_ref[hh] = s_scr[hh].T


def _scan(mode, lay, seg, direction, proj, fsrc, lb, s0, o_prev):
    M = _M(lay)
    if mode == "gla":
        H, dk, dv, Hb = GLA_HEADS, GLA_DK, GLA_DV, 1
        q_c, k_c, f_c, v_c = 0, GLA_HEADS, direction * GLA_HEADS, 2 * GLA_HEADS * GLA_DK // GLA_DV
    else:
        H, dk, dv, Hb = HG_HEADS, HG_DK, HG_DV, 4
        nb = HG_HEADS // Hb
        q_c, k_c, f_c, v_c = 0, None, (1 + direction) * nb, 3 * nb
    if seg == 0:
        B, L, boff = lay.B0, lay.L0, 0
    else:
        B, L, boff = lay.B1, lay.L1, _R0(lay) // lay.L1
        assert _R0(lay) % lay.L1 == 0
    reverse = direction == 1
    T = min(L, GLA_SCAN_ROWS)
    nT = L // T
    zero_init = s0 is None
    emit_state = seg == 0
    W = proj.shape[1]
    pv = proj.reshape(M // L, L, W)
    fv = fsrc.reshape(M // L, L, fsrc.shape[1])

    def tt(t):
        return (nT - 1 - t) if reverse else t

    def cspec(width, c0):
        return pl.BlockSpec((None, T, width), lambda b, h, t: (b + boff, tt(t), c0 + h))

    in_specs = [cspec(Hb * dk, q_c)]
    args = [pv]
    if mode == "gla":
        in_specs.append(cspec(Hb * dk, k_c))
        args.append(pv)
    in_specs.append(cspec(Hb * dk, f_c))
    args.append(fv)
    in_specs.append(cspec(Hb * dv, v_c))
    args.append(pv)
    if mode == "hgrn":
        in_specs.append(pl.BlockSpec((None, 1, Hb * dk), lambda b, h, t: (direction, 0, h)))
        args.append(lb)
    if not zero_init:
        in_specs.append(pl.BlockSpec((None, None, Hb, dk, dv), lambda b, h, t: (b, direction, h, 0, 0)))
        args.append(s0)
    aliases = {}
    if o_prev is not None:
        in_specs.append(pl.BlockSpec(memory_space=pl.ANY))
        args.append(o_prev.reshape(M // L, L, H * dv))
        aliases = {len(args) - 1: 0}
    out_specs = [pl.BlockSpec((None, T, Hb * dv), lambda b, h, t: (b + boff, tt(t), h))]
    out_shape = [jax.ShapeDtypeStruct((M // L, L, H * dv), F32)]
    if emit_state:
        out_specs.append(pl.BlockSpec((None, Hb, dk, dv), lambda b, h, t: (b, h, 0, 0)))
        out_shape.append(jax.ShapeDtypeStruct((B, H, dk, dv), F32))

    def kern(*refs):
        refs = list(refs)
        if o_prev is not None:
            n_in = len(args)
            refs.pop(n_in - 1)
        _scan_kernel(*refs, mode=mode, reverse=reverse, T=T, Hb=Hb, dk=dk, dv=dv,
                     zero_init=zero_init, emit_state=emit_state, nT=nT)

    outs = pl.pallas_call(
        kern,
        grid=(B, H // Hb, nT),
        in_specs=in_specs,
        out_specs=out_specs,
        out_shape=out_shape,
        scratch_shapes=[pltpu.VMEM((Hb, dv, dk), F32)],
        input_output_aliases=aliases,
        compiler_params=_cparams(3),
    )(*args)
    o = outs[0].reshape(M, H * dv)
    return o, (outs[1] if emit_state else None)


def _gated_norm_kernel(of_ref, ob_ref, g_ref, gain_ref, o_ref, *, H, dv):
    gain = gain_ref[...]
    for h in range(H):
        cs = slice(h * dv, (h + 1) * dv)
        o = of_ref[:, cs] + ob_ref[:, cs]
        o = o * lax.rsqrt(jnp.mean(o * o, axis=-1, keepdims=True) + EPS) * gain
        g = g_ref[:, cs]
        o_ref[:, cs] = (o * (g * _sigmoid(g))).astype(o_ref.dtype)


def _gated_norm(o_f, o_b, proj, g_col, gain, H, dv, tm):
    M = o_f.shape[0]
    return pl.pallas_call(
        functools.partial(_gated_norm_kernel, H=H, dv=dv),
        grid=(M // tm,),
        in_specs=[pl.BlockSpec((tm, D), lambda i: (i, 0)),
                  pl.BlockSpec((tm, D), lambda i: (i, 0)),
                  pl.BlockSpec((tm, D), lambda i: (i, g_col)),
                  pl.BlockSpec((1, dv), lambda i: (0, 0))],
        out_specs=pl.BlockSpec((tm, D), lambda i: (i, 0)),
        out_shape=jax.ShapeDtypeStruct((M, D), BF16),
        compiler_params=_cparams(1),
    )(o_f, o_b, proj, gain.reshape(1, dv))


def _bidir_scan(mode, lay, proj, fsrc, lb, s0_sample):
    outs, states = [], []
    for d in range(2):
        o, st = _scan(mode, lay, 0, d, proj, fsrc, lb, None, None)
        o, _ = _scan(mode, lay, 1, d, proj, fsrc, lb, s0_sample, o)
        outs.append(o)
        states.append(st)
    return outs[0], outs[1], jnp.stack(states, axis=1)


def _gla_mixer(lay, h, x, mgate, s0, w_in, gk_w1, gk_w2, gk_b, norm_g, w_out, tm):
    kd = GLA_HEADS * GLA_DK
    proj = _linear(h, [w_in], tm=tm, tn=1024)
    w1cat = jnp.zeros((D, LANES), F32).at[:, :GLA_GATE_RANK].set(gk_w1[0])
    w1cat = w1cat.at[:, GLA_GATE_RANK:2 * GLA_GATE_RANK].set(gk_w1[1])
    w2cat = jnp.zeros((LANES, 2 * kd), F32).at[:GLA_GATE_RANK, :kd].set(gk_w2[0])
    w2cat = w2cat.at[GLA_GATE_RANK:2 * GLA_GATE_RANK, kd:].set(gk_w2[1])
    low = _linear(h, [w1cat], tm=tm, tn=LANES)
    gk = _linear(low, [w2cat], tm=tm, tn=1024, bias=gk_b.reshape(2 * kd))
    o_f, o_b, new_state = _bidir_scan("gla", lay, proj, gk, None, s0)
    y = _gated_norm(o_f, o_b, proj, 2, norm_g, GLA_HEADS, GLA_DV, min(tm, 256))
    x = _linear(y, [w_out], tm=tm, tn=1024, res=x, modgate=mgate, lay=lay)
    return x, new_state


def _hgrn2_mixer(lay, h, x, mgate, s0, layer_idx, w_in, lb_raw, norm_g, w_out, tm):
    proj = _linear(h, [w_in], tm=tm, tn=1024)
    lb = jnp.cumsum(jax.nn.softmax(lb_raw.astype(F32), axis=1), axis=1)
    lb = (lb - lb[:, :1])[:, layer_idx].reshape(2, 1, D)
    o_f, o_b, new_state = _bidir_scan("hgrn", lay, proj, proj, lb, s0)
    y = _gated_norm(o_f, o_b, proj, 4, norm_g, HG_HEADS, HG_DV, min(tm, 256))
    x = _linear(y, [w_out], tm=tm, tn=1024, res=x, modgate=mgate, lay=lay)
    return x, new_state


def _seq_pos(lay, r):
    r0 = _R0(lay)
    is_p = r < r0
    pos = jnp.where(is_p, r % lay.L0, (r - r0) % lay.L1)
    return is_p, pos, jnp.where(is_p, lay.L0, lay.L1)


def _hy_conv_kernel(x_ref, xp_ref, xn_ref, w_ref, b_ref, o_ref, *, lay, tm):
    _, pos0, seq_len = _seq_pos(lay, pl.program_id(0) * tm)
    x = x_ref[...]
    rows = lax.broadcasted_iota(jnp.int32, (tm, 1), 0)
    prev_row = jnp.where(pos0 == 0, 0.0, xp_ref[7:8, :])
    next_row = jnp.where(pos0 + tm == seq_len, 0.0, xn_ref[0:1, :])
    x_m1 = jnp.where(rows == 0, prev_row, pltpu.roll(x, 1, 0))
    x_p1 = jnp.where(rows == tm - 1, next_row, pltpu.roll(x, tm - 1, 0))
    w = w_ref[...]
    o_ref[...] = x_m1 * w[0:1] + x * w[1:2] + x_p1 * w[2:3] + b_ref[...]


def _hy_conv(lay, x, w, b, tm, tc):
    M, W = x.shape
    sub = 8
    return pl.pallas_call(
        functools.partial(_hy_conv_kernel, lay=lay, tm=tm),
        grid=(M // tm, W // tc),
        in_specs=[pl.BlockSpec((tm, tc), lambda i, j: (i, j)),
                  pl.BlockSpec((sub, tc), lambda i, j: (jnp.maximum(i * (tm // sub) - 1, 0), j)),
                  pl.BlockSpec((sub, tc), lambda i, j: (jnp.minimum((i + 1) * (tm // sub), M // sub - 1), j)),
                  pl.BlockSpec((3, tc), lambda i, j: (0, j)),
                  pl.BlockSpec((1, tc), lambda i, j: (0, j))],
        out_specs=pl.BlockSpec((tm, tc), lambda i, j: (i, j)),
        out_shape=jax.ShapeDtypeStruct((M, W), F32),
        compiler_params=_cparams(2),
    )(x, x, x, w, b.reshape(1, W))


def _hy_geom(L):
    P = min(L, HY_BLOCK_MAX)
    return P, L // P, P + LANES


def _dft_mats(P):
    N, Pp = 2 * P, P + LANES
    k = jnp.arange(Pp, dtype=jnp.int32)[:, None]
    m = jnp.arange(N, dtype=jnp.int32)[None, :]
    ang = ((k * m) % N).astype(F32) * (2.0 * math.pi / N)
    valid = k <= P
    cos = jnp.where(valid, jnp.cos(ang), 0.0)
    sin = jnp.where(valid, jnp.sin(ang), 0.0)
    fwd = jnp.concatenate([cos, -sin], axis=0)
    ck = jnp.where((k == 0) | (k == P), 1.0, 2.0) / N
    inv = jnp.concatenate([(cos * ck)[:, :P].T, (-sin * ck)[:, :P].T], axis=1)
    return fwd.astype(BF16), inv.astype(BF16)


def _hy_filter_kernel(z_ref, w1_ref, b1_ref, w2_ref, b2_ref, fr_ref, w3_ref, dl_ref, o_ref, n_ref):
    z = z_ref[...]
    fr = fr_ref[...]
    hid = jnp.sin(fr * (jnp.dot(z, w1_ref[...], precision=HIGHEST, preferred_element_type=F32) + b1_ref[...]))
    hid = jnp.sin(fr * (jnp.dot(hid, w2_ref[...], precision=HIGHEST, preferred_element_type=F32) + b2_ref[...]))
    f = jnp.dot(hid, w3_ref[...], precision=HIGHEST, preferred_element_type=F32)
    f = f * jnp.exp(-z[:, 0:1] * dl_ref[...]) * z[:, HY_FEAT_VALID:HY_FEAT_VALID + 1]
    o_ref[...] = f

    @pl.when(pl.program_id(1) == 0)
    def _():
        n_ref[...] = jnp.zeros(n_ref.shape, F32)
    n_ref[...] += jnp.sum(jnp.abs(f), axis=0, keepdims=True)


HY_FEAT_VALID = 1 + 2 * HY_BANDS


def _hy_filters(L, w1, b1, w2, b2, w3, freq, tm, tn):
    i = jnp.arange(2 * L, dtype=jnp.int32)
    p = jnp.where(i < L, L - i, i - L).astype(F32)
    t = p / L
    bands = jnp.arange(1, HY_BANDS + 1, dtype=F32)
    ang = (2.0 * math.pi / L) * p[:, None] * bands[None, :]
    feats = jnp.concatenate([t[:, None], jnp.cos(ang), jnp.sin(ang), (i > 0).astype(F32)[:, None]], axis=-1)
    z = jnp.zeros((2 * L, LANES), F32).at[:, :HY_FEAT_VALID + 1].set(feats)
    w1p = jnp.zeros((LANES, HY_WIDTH), F32).at[:HY_FEAT_VALID].set(w1)
    deltas = jnp.abs(jnp.linspace(math.log(HY_TARGET) / HY_SLOW, math.log(HY_TARGET) / HY_FAST, D, dtype=F32))
    nj = D // tn
    n_anti = L // tm
    return pl.pallas_call(
        _hy_filter_kernel,
        grid=(HY_ORDER * nj, 2 * L // tm),
        in_specs=[pl.BlockSpec((tm, LANES), lambda j, i: (i, 0)),
                  pl.BlockSpec((LANES, HY_WIDTH), lambda j, i: (0, 0)),
                  pl.BlockSpec((1, HY_WIDTH), lambda j, i: (0, 0)),
                  pl.BlockSpec((HY_WIDTH, HY_WIDTH), lambda j, i: (0, 0)),
                  pl.BlockSpec((1, HY_WIDTH), lambda j, i: (0, 0)),
                  pl.BlockSpec((1, HY_WIDTH), lambda j, i: (0, 0)),
                  pl.BlockSpec((HY_WIDTH, tn),
                               lambda j, i: (0, (j // nj) * 2 * nj + jnp.where(i < n_anti, nj, 0) + j % nj)),
                  pl.BlockSpec((1, tn), lambda j, i: (0, j % nj))],
        out_specs=[pl.BlockSpec((tm, tn), lambda j, i: (i, j)),
                   pl.BlockSpec((1, tn), lambda j, i: (0, j))],
        out_shape=[jax.ShapeDtypeStruct((2 * L, HY_ORDER * D), F32),
                   jax.ShapeDtypeStruct((1, HY_ORDER * D), F32)],
        compiler_params=_cparams(2),
    )(z, w1p, b1.reshape(1, -1), w2, b2.reshape(1, -1), freq.reshape(1, -1), w3, deltas.reshape(1, D))


def _bmm_kernel(*refs, n_pair, has_scale, has_gate):
    it = iter(refs)
    a_refs = [next(it) for _ in range(n_pair)]
    x_refs = [next(it) for _ in range(n_pair)]
    sc_ref = next(it) if has_scale else None
    if has_gate:
        g_ref, z_ref, zb_ref = next(it), next(it), next(it)
    o_ref = next(it)
    acc = None
    for a, x in zip(a_refs, x_refs):
        p = jnp.dot(a[...], x[...].astype(BF16), preferred_element_type=F32)
        acc = p if acc is None else acc + p
    if has_scale:
        acc = acc / sc_ref[...]
    if has_gate:
        acc = g_ref[...] * (acc + z_ref[...] * zb_ref[...])
    o_ref[...] = acc.astype(o_ref.dtype)


def _bmm(a_list, x_list, *, nblk, tr, tn, out_rows, out_cols, out_dtype, out_nblk=None, out_boff=0,
         colscale=None, gate=None, zin=None, zbias=None, o_prev=None):
    out_nblk = nblk if out_nblk is None else out_nblk
    in_specs, args = [], []
    for a, cb, kb in a_list:
        in_specs.append(pl.BlockSpec((tr, kb), lambda b, i, j, cb=cb: (i, cb)))
        args.append(a)
    for (x, boff, coff), (_, _, kb) in zip(x_list, a_list):
        in_specs.append(pl.BlockSpec((None, kb, tn), lambda b, i, j, boff=boff, coff=coff: (b + boff, 0, coff + j)))
        args.append(x)
    if colscale is not None:
        in_specs.append(pl.BlockSpec((1, tn), lambda b, i, j: (0, j)))
        args.append(colscale)
    if gate is not None:
        for arr, boff, coff in (gate, zin):
            in_specs.append(pl.BlockSpec((None, tr, tn),
                                         lambda b, i, j, boff=boff, coff=coff: (b + boff, i, coff + j)))
            args.append(arr)
        in_specs.append(pl.BlockSpec((1, tn), lambda b, i, j: (0, j)))
        args.append(zbias)
    aliases = {}
    n_real = len(args)
    if o_prev is not None:
        in_specs.append(pl.BlockSpec(memory_space=pl.ANY))
        args.append(o_prev)
        aliases = {n_real: 0}

    def kern(*refs):
        refs = list(refs)
        if o_prev is not None:
            refs.pop(n_real)
        _bmm_kernel(*refs, n_pair=len(a_list), has_scale=colscale is not None, has_gate=gate is not None)

    return pl.pallas_call(
        kern,
        grid=(nblk, out_rows // tr, out_cols // tn),
        in_specs=in_specs,
        out_specs=pl.BlockSpec((None, tr, tn), lambda b, i, j: (b + out_boff, i, j)),
        out_shape=jax.ShapeDtypeStruct((out_nblk, out_rows, out_cols), out_dtype),
        input_output_aliases=aliases,
        compiler_params=_cparams(3),
    )(*args)


def _hy_mac_kernel(ure_ref, uim_ref, gre_ref, gim_ref, yre_ref, yim_ref, *, nb):
    for i in range(nb):
        acc_re = acc_im = None
        for j in range(nb):
            dd = i - j + nb - 1
            ur, ui = ure_ref[j], uim_ref[j]
            gr, gi = gre_ref[dd], gim_ref[dd]
            re = gr * ur - gi * ui
            im = gr * ui + gi * ur
            acc_re = re if acc_re is None else acc_re + re
            acc_im = im if acc_im is None else acc_im + im
        yre_ref[i] = acc_re.astype(yre_ref.dtype)
        yim_ref[i] = acc_im.astype(yim_ref.dtype)


def _hy_mac(u, g, order, B, nb, Pp, tc):
    tr = 384 if Pp % 384 == 0 else (256 if Pp % 256 == 0 else Pp)
    nr = Pp // tr
    nd = 2 * nb - 1
    u4 = u.reshape(B, nb, 2 * Pp, D)
    ospec = pl.BlockSpec((None, nb, tr, tc), lambda i, j, b: (b, 0, i, j))
    oshape = jax.ShapeDtypeStruct((B, nb, Pp, D), BF16)
    yre, yim = pl.pallas_call(
        functools.partial(_hy_mac_kernel, nb=nb),
        grid=(nr, D // tc, B),
        in_specs=[pl.BlockSpec((None, nb, tr, tc), lambda i, j, b: (b, 0, i, j)),
                  pl.BlockSpec((None, nb, tr, tc), lambda i, j, b: (b, 0, i + nr, j)),
                  pl.BlockSpec((nd, tr, tc), lambda i, j, b: (0, i, order * (D // tc) + j)),
                  pl.BlockSpec((nd, tr, tc), lambda i, j, b: (0, i + nr, order * (D // tc) + j))],
        out_specs=[ospec, ospec],
        out_shape=[oshape, oshape],
        compiler_params=_cparams(3),
    )(u4, u4, g, g)
    return yre.reshape(B * nb, Pp, D), yim.reshape(B * nb, Pp, D)


def _hyena_mixer(lay, h, x, mgate, w_in, conv_w, conv_b, f_w1, f_b1, f_w2, f_b2, f_w3, f_freq, f_bias,
                 w_out, tm):
    M = _M(lay)
    proj = _linear(h, [w_in], tm=tm, tn=1024)
    cv = _hy_conv(lay, proj, conv_w, conv_b, min(tm, lay.L0, 512), 512)
    z_all = None
    for seg in (0, 1):
        B, L, row_off = (lay.B0, lay.L0, 0) if seg == 0 else (lay.B1, lay.L1, _R0(lay))
        P, nb, Pp = _hy_geom(L)
        assert row_off % P == 0
        boff = row_off // P
        tn = D if P <= 256 else 512
        nj = D // tn
        fwd, inv = _dft_mats(P)
        fext, nrm = _hy_filters(L, f_w1, f_b1, f_w2, f_b2, f_w3, f_freq, min(L, 512), 512)
        fx = fext.reshape(2 * nb, P, HY_ORDER * D)
        g = _bmm([(fwd, 0, P), (fwd, 1, P)], [(fx, 1, 0), (fx, 0, 0)], nblk=2 * nb - 1, tr=Pp, tn=tn,
                 out_rows=2 * Pp, out_cols=HY_ORDER * D, out_dtype=F32, colscale=nrm)
        cvv = cv.reshape(M // P, P, 3 * D)
        zsrc = (cvv, boff, 2 * nj)
        for n in range(HY_ORDER):
            u = _bmm([(fwd, 0, P)], [zsrc], nblk=B * nb, tr=Pp, tn=tn, out_rows=2 * Pp, out_cols=D,
                     out_dtype=F32)
            yre, yim = _hy_mac(u, g, n, B, nb, Pp, 1024 if nb == 1 else 256)
            last = n == HY_ORDER - 1
            z = _bmm([(inv, 0, Pp), (inv, 1, Pp)], [(yre, 0, 0), (yim, 0, 0)], nblk=B * nb, tr=min(P, 512),
                     tn=tn, out_rows=P, out_cols=D, out_dtype=F32,
                     out_nblk=(M // P) if last else None, out_boff=boff if last else 0,
                     gate=(cvv, boff, n * nj), zin=zsrc, zbias=f_bias[n].reshape(1, D),
                     o_prev=None if (not last or z_all is None) else z_all.reshape(M // P, P, D))
            zsrc = (z, boff if last else 0, 0)
        z_all = z.reshape(M, D)
    return _linear(z_all, [w_out], tm=tm, tn=1024, res=x, modgate=mgate, lay=lay)


def _rw_shift_kernel(x_ref, xp_ref, xn_ref, mu_ref, o_ref, *, lay, tm):
    j = pl.program_id(1)
    is_p, pos0, _ = _seq_pos(lay, pl.program_id(0) * tm)
    x = x_ref[...]
    hp = xp_ref[...]
    hn = xn_ref[...]
    rows = lax.broadcasted_iota(jnp.int32, (tm, 1), 0)
    pos = pos0 + rows
    col = pos & (GRID_W - 1)
    x_m1 = jnp.where(rows == 0, hp[GRID_W - 1:GRID_W], pltpu.roll(x, 1, 0))
    x_p1 = jnp.where(rows == tm - 1, hn[0:1], pltpu.roll(x, tm - 1, 0))
    if tm > GRID_W:
        x_mw = jnp.concatenate([hp, x[:tm - GRID_W]], axis=0)
        x_pw = jnp.concatenate([x[GRID_W:], hn], axis=0)
    else:
        x_mw, x_pw = hp, hn
    ok_m1 = jnp.where(is_p, pos, col) != 0
    ok_p1 = jnp.where(is_p, pos - (lay.L0 - 1), col - (GRID_W - 1)) != 0
    s_m1 = jnp.where(ok_m1, x_m1, 0.0)
    s_p1 = jnp.where(ok_p1, x_p1, 0.0)
    s_mw = jnp.where(pos >= GRID_W, x_mw, 0.0)
    s_pw = jnp.where(pos < lay.L1 - GRID_W, x_pw, 0.0)
    n_m1 = jnp.where(is_p, 2, 1)
    n_1 = jnp.where(is_p, 4, 2)
    sh = jnp.where(j < n_m1, s_m1, jnp.where(j < n_1, s_p1, jnp.where(j == 2, s_mw, s_pw)))
    xx = sh - x
    mu = mu_ref[...]
    for k in range(6):
        o_ref[k] = (x + xx * mu[k:k + 1]).astype(o_ref.dtype)


def _rw_shift(lay, h, mu, tm):
    M = h.shape[0]
    tc = D // 4
    g = GRID_W
    return pl.pallas_call(
        functools.partial(_rw_shift_kernel, lay=lay, tm=tm),
        grid=(M // tm, 4),
        in_specs=[pl.BlockSpec((tm, tc), lambda i, j: (i, j)),
                  pl.BlockSpec((g, tc), lambda i, j: (jnp.maximum(i * (tm // g) - 1, 0), j)),
                  pl.BlockSpec((g, tc), lambda i, j: (jnp.minimum((i + 1) * (tm // g), M // g - 1), j)),
                  pl.BlockSpec((6, tc), lambda i, j: (0, j))],
        out_specs=pl.BlockSpec((6, tm, tc), lambda i, j: (0, i, j)),
        out_shape=jax.ShapeDtypeStruct((6, M, D), BF16),
        compiler_params=_cparams(2),
    )(h, h, h, mu)


def _head_sum(x, seg):
    hi = x.astype(BF16)
    lo = (x - hi.astype(F32)).astype(BF16)
    return (jnp.dot(hi, seg, preferred_element_type=F32) + jnp.dot(lo, seg, preferred_element_type=F32))


def _head_seg(tc):
    r = lax.broadcasted_iota(jnp.int32, (tc, tc), 0) // RW_HEAD
    c = lax.broadcasted_iota(jnp.int32, (tc, tc), 1) // RW_HEAD
    return (r == c).astype(BF16)


def _rw_prep_kernel(r_ref, k_ref, v_ref, lw0_ref, lw1_ref, ar0_ref, ar1_ref, w0_ref, a0_ref, kk_p_ref,
                    ka_ref, rk_ref, kk_ref, w0o_ref, w1o_ref, kd0_ref, kd1_ref, b0_ref, b1_ref, bon_ref, *, tc):
    seg = _head_seg(tc)
    r, k, v = r_ref[...], k_ref[...], v_ref[...]
    kk = k * kk_p_ref[...]
    kk = kk * lax.rsqrt(_head_sum(kk * kk, seg) + 1e-12)
    kk_ref[...] = kk
    bonus = None
    for d, (lw_ref, ar_ref, wo_ref, kdo_ref, bo_ref) in enumerate(
            ((lw0_ref, ar0_ref, w0o_ref, kd0_ref, b0_ref), (lw1_ref, ar1_ref, w1o_ref, kd1_ref, b1_ref))):
        wo_ref[...] = -jnp.exp(-_softplus(-(w0_ref[d:d + 1, :] + lw_ref[...])) - 0.5)
        a = _sigmoid(a0_ref[d:d + 1, :] + ar_ref[...])
        kd = k * (1.0 + (a - 1.0) * ka_ref[...])
        kdo_ref[...] = kd
        bo_ref[...] = a * kk
        bon = _head_sum(r * kd * rk_ref[...], seg) * v
        bonus = bon if bonus is None else bonus + bon
    bon_ref[...] = bonus


def _rw_prep(r, k, v, lw0, lw1, ar0, ar1, w0, a0, k_k, k_a, r_k, tm, tc):
    M = r.shape[0]
    big = pl.BlockSpec((tm, tc), lambda i, j: (i, j))
    two = pl.BlockSpec((2, tc), lambda i, j: (0, j))
    one = pl.BlockSpec((1, tc), lambda i, j: (0, j))
    shape = jax.ShapeDtypeStruct((M, D), F32)
    return pl.pallas_call(
        functools.partial(_rw_prep_kernel, tc=tc),
        grid=(M // tm, D // tc),
        in_specs=[big] * 7 + [two, two, one, one, one],
        out_specs=[big] * 8,
        out_shape=[shape] * 8,
        compiler_params=_cparams(2),
    )(r, k, v, lw0, lw1, ar0, ar1, w0, a0, k_k.reshape(1, D), k_a.reshape(1, D), r_k.reshape(1, D))


def _rw_scan_kernel(*refs, reverse, T, Hb, zero_init, emit_state, nT):
    it = iter(refs)
    r_ref, w_ref, k_ref, v_ref, kk_ref, b_ref = (next(it) for _ in range(6))
    s0_ref = None if zero_init else next(it)
    y_ref = next(it)
    so_ref = next(it) if emit_state else None
    s_scr = next(it)
    C, Kd = CHUNK, RW_HEAD
    P2 = 2 * Kd
    npair = Hb // 2
    t = pl.program_id(2)
    zero_blk = jnp.zeros((Kd, Kd), F32)

    @pl.when(t == 0)
    def _():
        for p in range(npair):
            if zero_init:
                s_scr[p] = jnp.zeros((P2, P2), F32)
            else:
                s_scr[p] = jnp.concatenate(
                    [jnp.concatenate([s0_ref[2 * p], zero_blk], axis=1),
                     jnp.concatenate([zero_blk, s0_ref[2 * p + 1]], axis=1)], axis=0)

    row = lax.broadcasted_iota(jnp.int32, (C, C), 0)
    col = lax.broadcasted_iota(jnp.int32, (C, C), 1)
    tri = ((row <= col) if reverse else (row >= col)).astype(F32)
    row2 = lax.broadcasted_iota(jnp.int32, (P2, P2), 0)
    col2 = lax.broadcasted_iota(jnp.int32, (P2, P2), 1)
    same = (row2 // C) == (col2 // C)
    tr, tc_ = row2 % C, col2 % C
    incl2 = same & ((tr <= tc_) if reverse else (tr >= tc_))
    strict2 = same & ((tr < tc_) if reverse else (tr > tc_))
    eye2 = (row2 == col2).astype(F32)
    lane = lax.broadcasted_iota(jnp.int32, (C, P2), 1)
    h0 = lane < Kd
    last_i = 0 if reverse else C - 1
    nc = T // C
    nt_dims = (((1,), (1,)), ((), ()))
    tn_dims = (((0,), (0,)), ((), ()))

    def mm(a, b):
        return jnp.dot(a.astype(BF16), b.astype(BF16), preferred_element_type=F32)

    def stack(x):
        return jnp.concatenate([jnp.where(h0, x, 0.0), jnp.where(h0, 0.0, x)], axis=0)

    def fold(x2):
        return x2[:C] + x2[C:]

    states = [s_scr[p] for p in range(npair)]
    order = [(nc - 1 - i) if reverse else i for i in range(nc)]
    items = [(cc, p) for cc in order for p in range(npair)]
    wide = {}
    for cc in order:
        rows = slice(cc * C, (cc + 1) * C)
        lw = w_ref[rows, :]
        cum = jnp.dot(tri, lw, precision=HIGHEST, preferred_element_type=F32)
        tot = cum[last_i:last_i + 1, :]
        e_neg = jnp.exp(-cum)
        e_rem = jnp.exp(tot - cum)
        kd_, b_ = k_ref[rows, :], b_ref[rows, :]
        wide[cc] = dict(gam=jnp.exp(tot), alpha=kk_ref[rows, :] * jnp.exp(cum - lw),
                        rho=r_ref[rows, :] * jnp.exp(cum), beta=b_ * e_neg, kappa=kd_ * e_neg,
                        bet2=b_ * e_rem, kap2=kd_ * e_rem, v=v_ref[rows, :])

    def pair(cc, p, name):
        return wide[cc][name][:, p * P2:(p + 1) * P2]

    a2 = [stack(pair(cc, p, "alpha")) for cc, p in items]
    r2 = [stack(pair(cc, p, "rho")) for cc, p in items]
    v2 = [stack(pair(cc, p, "v")) for cc, p in items]
    big = [lax.dot_general(
        jnp.concatenate([a, r], axis=0).astype(BF16),
        jnp.concatenate([stack(pair(cc, p, "beta")), stack(pair(cc, p, "kappa"))], axis=0).astype(BF16),
        nt_dims, preferred_element_type=F32) for a, r, (cc, p) in zip(a2, r2, items)]
    xk = [jnp.where(strict2, -g[:P2, :P2], 0.0) for g in big]
    l_ak = [jnp.where(strict2, g[:P2, P2:], 0.0) for g in big]
    m_rb = [jnp.where(incl2, g[P2:, :P2], 0.0) for g in big]
    m_rk = [jnp.where(incl2, g[P2:, P2:], 0.0) for g in big]
    tinv = [eye2 + x for x in xk]
    for _ in range(5):
        xk = [mm(x, x) for x in xk]
        tinv = [tv + mm(tv, x) for tv, x in zip(tinv, xk)]
    lv2 = [mm(l, v) for l, v in zip(l_ak, v2)]
    au = [mm(tv, jnp.concatenate([a, lv], axis=1)) for tv, a, lv in zip(tinv, a2, lv2)]
    mau = [mm(m, x) for m, x in zip(m_rb, au)]
    mv = [mm(m, v) for m, v in zip(m_rk, v2)]
    r_t = [fold(r - ma[:, :P2]).astype(BF16) for r, ma in zip(r2, mau)]
    y0 = [fold(m - ma[:, P2:]) for m, ma in zip(mv, mau)]
    g_low, h_add = [], []
    for x, (cc, p) in zip(au, items):
        a_t, u_t = fold(x[:, :P2]), fold(x[:, P2:])
        bet2 = pair(cc, p, "bet2")
        g_low.append(jnp.where(same, lax.dot_general(a_t.astype(BF16), bet2.astype(BF16), tn_dims,
                                                     preferred_element_type=F32), 0.0).astype(BF16))
        vu = jnp.concatenate([pair(cc, p, "v"), u_t], axis=0).astype(BF16)
        kb = jnp.concatenate([pair(cc, p, "kap2"), -bet2], axis=0).astype(BF16)
        h_add.append(jnp.where(same, lax.dot_general(vu, kb, tn_dims, preferred_element_type=F32), 0.0))
    y_rows = [None] * nc
    for n, cc in enumerate(order):
        idx = [n * npair + p for p in range(npair)]
        sb = [s.astype(BF16) for s in states]
        y_pairs = [y0[i] + lax.dot_general(r_t[i], sb[p], nt_dims, preferred_element_type=F32)
                   for p, i in enumerate(idx)]
        states = [states[p] * pair(cc, p, "gam") - jnp.dot(sb[p], g_low[i], preferred_element_type=F32)
                  + h_add[i] for p, i in enumerate(idx)]
        y_rows[cc] = jnp.concatenate(y_pairs, axis=1) if npair > 1 else y_pairs[0]
    y_ref[...] = jnp.concatenate(y_rows, axis=0) if nc > 1 else y_rows[0]
    for p in range(npair):
        s_scr[p] = states[p]

    if emit_state:
        @pl.when(t == nT - 1)
        def _():
            for p in range(npair):
                so_ref[2 * p] = states[p][:Kd, :Kd]
                so_ref[2 * p + 1] = states[p][Kd:, Kd:]


def _rw_scan(lay, seg, direction, r, lw, kd, v, kk, b, s0, y_prev):
    M = _M(lay)
    Hb = RW_SCAN_HEADS
    H = RW_HEADS
    if seg == 0:
        B, L, boff = lay.B0, lay.L0, 0
    else:
        B, L, boff = lay.B1, lay.L1, _R0(lay) // lay.L1
    reverse = direction == 1
    T = min(L, RW_SCAN_ROWS)
    nT = L // T
    zero_init = s0 is None
    emit_state = seg == 0
    wd = Hb * RW_HEAD

    def tt(t):
        return (nT - 1 - t) if reverse else t

    spec = pl.BlockSpec((None, T, wd), lambda bb, h, t: (bb + boff, tt(t), h))
    args = [a.reshape(M // L, L, D) for a in (r, lw, kd, v, kk, b)]
    in_specs = [spec] * 6
    if not zero_init:
        in_specs.append(pl.BlockSpec((None, None, Hb, RW_HEAD, RW_HEAD),
                                     lambda bb, h, t: (bb, direction, h, 0, 0)))
        args.append(s0)
    aliases = {}
    n_real = len(args)
    if y_prev is not None:
        in_specs.append(pl.BlockSpec(memory_space=pl.ANY))
        args.append(y_prev.reshape(M // L, L, D))
        aliases = {n_real: 0}
    out_specs = [spec]
    out_shape = [jax.ShapeDtypeStruct((M // L, L, D), F32)]
    if emit_state:
        out_specs.append(pl.BlockSpec((None, Hb, RW_HEAD, RW_HEAD), lambda bb, h, t: (bb, h, 0, 0)))
        out_shape.append(jax.ShapeDtypeStruct((B, H, RW_HEAD, RW_HEAD), F32))

    def kern(*refs):
        refs = list(refs)
        if y_prev is not None:
            refs.pop(n_real)
        _rw_scan_kernel(*refs, reverse=reverse, T=T, Hb=Hb, zero_init=zero_init, emit_state=emit_state, nT=nT)

    outs = pl.pallas_call(
        kern,
        grid=(B, H // Hb, nT),
        in_specs=in_specs,
        out_specs=out_specs,
        out_shape=out_shape,
        scratch_shapes=[pltpu.VMEM((Hb // 2, 2 * RW_HEAD, 2 * RW_HEAD), F32)],
        input_output_aliases=aliases,
        compiler_params=_cparams(3),
    )(*args)
    return outs[0].reshape(M, D), (outs[1] if emit_state else None)


def _rw_post_kernel(y0_ref, y1_ref, bon_ref, g_ref, lnw_ref, lnb_ref, o_ref, *, tc):
    seg = _head_seg(tc)
    y = y0_ref[...] + y1_ref[...]
    mean = _head_sum(y, seg) * (1.0 / RW_HEAD)
    yc = y - mean
    var = _head_sum(yc * yc, seg) * (1.0 / RW_HEAD)
    y = yc * lax.rsqrt(var + RW_GN_EPS) * lnw_ref[...] + lnb_ref[...]
    o_ref[...] = ((y + bon_ref[...]) * g_ref[...]).astype(o_ref.dtype)


def _rw_post(y0, y1, bonus, gate, ln_w, ln_b, tm, tc):
    M = y0.shape[0]
    big = pl.BlockSpec((tm, tc), lambda i, j: (i, j))
    one = pl.BlockSpec((1, tc), lambda i, j: (0, j))
    return pl.pallas_call(
        functools.partial(_rw_post_kernel, tc=tc),
        grid=(M // tm, D // tc),
        in_specs=[big] * 4 + [one, one],
        out_specs=big,
        out_shape=jax.ShapeDtypeStruct((M, D), BF16),
        compiler_params=_cparams(2),
    )(y0, y1, bonus, gate, ln_w.reshape(1, D), ln_b.reshape(1, D))


def _rwkv7_mixer(lay, h, x, mgate, s0, mu, w_r, w_k, w_v, w_o, w0, w1, w2, a0, a1, a2, g1, g2,
                 k_k, k_a, r_k, ln_w, ln_b, tm):
    x6 = _rw_shift(lay, h, mu, min(lay.L0, 256))
    r = _linear(x6, [w_r], x_idx=0, tm=tm, tn=1024)
    k = _linear(x6, [w_k], x_idx=2, tm=tm, tn=1024)
    v = _linear(x6, [w_v], x_idx=3, tm=tm, tn=1024)
    gh = _linear(x6, [g1], x_idx=5, tm=tm, tn=256, act="sigmoid", out_dtype=BF16)
    gate = _linear(gh, [g2], tm=tm, tn=1024)
    rk = RW_RANK

    def cat_in(w):
        out = jnp.zeros((D, 2 * LANES), F32)
        return out.at[:, :rk].set(w[0]).at[:, LANES:LANES + rk].set(w[1])

    def pad_out(w, d):
        return jnp.zeros((2 * LANES, D), F32).at[d * LANES:d * LANES + rk].set(w)

    tw = _linear(x6, [cat_in(w1)], x_idx=1, tm=tm, tn=2 * LANES, act="tanh", out_dtype=BF16)
    ta = _linear(x6, [cat_in(a1)], x_idx=4, tm=tm, tn=2 * LANES)
    lws = [_linear(tw, [pad_out(w2[d], d)], tm=tm, tn=1024) for d in range(2)]
    ars = [_linear(ta, [pad_out(a2[d], d)], tm=tm, tn=1024) for d in range(2)]
    kk, lw0, lw1, kd0, kd1, b0, b1, bonus = _rw_prep(r, k, v, lws[0], lws[1], ars[0], ars[1], w0, a0,
                                                     k_k, k_a, r_k, min(tm, 512), 256)
    ys, states = [], []
    for d, (lw, kd, b) in enumerate(((lw0, kd0, b0), (lw1, kd1, b1))):
        y, st = _rw_scan(lay, 0, d, r, lw, kd, v, kk, b, None, None)
        y, _ = _rw_scan(lay, 1, d, r, lw, kd, v, kk, b, s0, y)
        ys.append(y)
        states.append(st)
    yo = _rw_post(ys[0], ys[1], bonus, gate, ln_w, ln_b, min(tm, 512), 256)
    x = _linear(yo, [w_o], tm=tm, tn=1024, res=x, modgate=mgate, lay=lay)
    return x, jnp.stack(states, axis=1)


def _router_kernel(x_ref, w_ref, o_ref, *, n_exp):
    logits = jnp.dot(x_ref[...].astype(F32), w_ref[...], precision=HIGHEST, preferred_element_type=F32)
    lane = lax.broadcasted_iota(jnp.int32, logits.shape, 1)
    neg = -jnp.inf
    lg = jnp.where(lane < n_exp, logits, neg)
    m1 = jnp.max(lg, axis=-1, keepdims=True)
    i1 = jnp.min(jnp.where(lg == m1, lane, LANES), axis=-1, keepdims=True)
    lg2 = jnp.where(lane == i1, neg, lg)
    m2 = jnp.max(lg2, axis=-1, keepdims=True)
    i2 = jnp.min(jnp.where(lg2 == m2, lane, LANES), axis=-1, keepdims=True)
    e = jnp.exp(m2 - m1)
    p1 = 1.0 / (1.0 + e)
    o_ref[...] = jnp.where(lane == i1, p1, 0.0) + jnp.where(lane == i2, e * p1, 0.0)


def _router(h, router, tm):
    M = h.shape[0]
    n_exp = router.shape[1]
    wp = jnp.zeros((D, LANES), F32).at[:, :n_exp].set(router)
    return pl.pallas_call(
        functools.partial(_router_kernel, n_exp=n_exp),
        grid=(M // tm,),
        in_specs=[pl.BlockSpec((tm, D), lambda i: (i, 0)), pl.BlockSpec((D, LANES), lambda i: (0, 0))],
        out_specs=pl.BlockSpec((tm, LANES), lambda i: (i, 0)),
        out_shape=jax.ShapeDtypeStruct((M, LANES), F32),
        compiler_params=_cparams(1),
    )(h, wp)


def _moe_plan(gates, n_exp, tb, sub, grp):
    i32 = jnp.int32
    M = gates.shape[0]
    nblk = M // tb
    g3 = gates[:, :n_exp].reshape(nblk, tb, n_exp)
    sel = g3 > 0.0
    rank = jnp.cumsum(sel.astype(i32), axis=1) - 1
    pos_t = jnp.where(sel, rank, -1).transpose(0, 2, 1)
    gate_t = g3.transpose(0, 2, 1)
    ns = (jnp.sum(sel.astype(i32), axis=1) + sub - 1) // sub
    ns_t = ns.T
    tot_e = jnp.sum(ns_t, axis=1)
    pad_e = (tot_e + grp - 1) // grp * grp
    end_e = jnp.cumsum(pad_e)
    base_e = end_e - pad_e
    cum_eb = jnp.cumsum(ns_t, axis=1)
    off_eb = cum_eb - ns_t
    nt = (2 * M) // sub + nblk * n_exp + n_exp * (grp - 1)
    nt = (nt + grp - 1) // grp * grp
    k = jnp.arange(nt, dtype=i32)
    e_k = jnp.minimum(jnp.sum((end_e[None, :] <= k[:, None]).astype(i32), axis=1), n_exp - 1)
    local = k - base_e[e_k]
    valid = (local < tot_e[e_k]) & (k < end_e[-1])
    b_k = jnp.minimum(jnp.sum((cum_eb[e_k] <= local[:, None]).astype(i32), axis=1), nblk - 1)
    s_k = local - off_eb[e_k, b_k]
    disp = (jnp.where(valid, b_k, nblk - 1), e_k, jnp.where(valid, s_k, 0), valid.astype(i32))
    ffn = (e_k[::grp], valid[::grp].astype(i32))
    ns_f = ns.reshape(-1)
    cum_f = jnp.cumsum(ns_f)
    exc_f = cum_f - ns_f
    pair = jnp.minimum(jnp.sum((cum_f[None, :] <= k[:, None]).astype(i32), axis=1), nblk * n_exp - 1)
    cvalid = k < cum_f[-1]
    cb = jnp.where(cvalid, pair // n_exp, nblk - 1)
    ce = jnp.where(cvalid, pair % n_exp, 0)
    cs = jnp.where(cvalid, k - exc_f[pair], 0)
    slot = jnp.where(cvalid, base_e[ce] + off_eb[ce, cb] + cs, 0)
    blk_tot = jnp.sum(ns, axis=1)
    blk_start = jnp.cumsum(blk_tot) - blk_tot
    first = cvalid & (k == blk_start[cb])
    comb = (cb, slot, ce, cs, cvalid.astype(i32), first.astype(i32))
    return pos_t, gate_t, nt, disp, ffn, comb


def _moe_dispatch_kernel(tb_ref, te_ref, ts_ref, tv_ref, h_ref, pos_ref, g_ref, xs_ref, gs_ref, *, sub):
    k = pl.program_id(0)
    e = te_ref[k]
    prow = jnp.where(tv_ref[k] > 0, pos_ref[pl.ds(e, 1), :], -2)
    grow = g_ref[pl.ds(e, 1), :]
    tgt = lax.broadcasted_iota(jnp.int32, (sub, prow.shape[1]), 0) + ts_ref[k] * sub
    hit = prow == tgt
    onehot = jnp.where(hit, 1.0, 0.0).astype(BF16)
    xs_ref[...] = jnp.dot(onehot, h_ref[...], preferred_element_type=F32).astype(xs_ref.dtype)
    gate = jnp.sum(jnp.where(hit, grow, 0.0), axis=1, keepdims=True)
    gs_ref[...] = jnp.broadcast_to(gate, gs_ref.shape)


def _moe_dispatch(h, pos_t, gate_t, nt, disp, tb, sub):
    n_exp = pos_t.shape[1]
    blk = lambda k, b, e, s, v: (b[k], 0)
    blk3 = lambda k, b, e, s, v: (b[k], 0, 0)
    out = lambda k, b, e, s, v: (k, 0)
    return pl.pallas_call(
        functools.partial(_moe_dispatch_kernel, sub=sub),
        grid_spec=pltpu.PrefetchScalarGridSpec(
            num_scalar_prefetch=4, grid=(nt,),
            in_specs=[pl.BlockSpec((tb, D), blk),
                      pl.BlockSpec((None, n_exp, tb), blk3),
                      pl.BlockSpec((None, n_exp, tb), blk3)],
            out_specs=[pl.BlockSpec((sub, D), out), pl.BlockSpec((sub, LANES), out)]),
        out_shape=[jax.ShapeDtypeStruct((nt * sub, D), BF16), jax.ShapeDtypeStruct((nt * sub, LANES), F32)],
        compiler_params=_cparams(1),
    )(*disp, h, pos_t, gate_t)


def _moe_up_kernel(te_ref, tv_ref, x_ref, w1_ref, w3_ref, o_ref):
    @pl.when(tv_ref[pl.program_id(1)] > 0)
    def _():
        x = x_ref[...]
        a = jnp.dot(x, w1_ref[...], preferred_element_type=F32)
        b = jnp.dot(x, w3_ref[...], preferred_element_type=F32)
        o_ref[...] = (a * _sigmoid(a) * b).astype(o_ref.dtype)


def _moe_down_kernel(te_ref, tv_ref, a_ref, w2_ref, gs_ref, o_ref):
    @pl.when(tv_ref[pl.program_id(1)] > 0)
    def _():
        y = jnp.dot(a_ref[...], w2_ref[...], preferred_element_type=F32)
        o_ref[...] = (y * gs_ref[:, 0:1]).astype(o_ref.dtype)


def _moe_ffn(xs, gs, w1, w3, w2, e0, ffn, tmf, tf, tn):
    rows = xs.shape[0]
    F = w1.shape[-1]
    ni = rows // tmf
    a = pl.pallas_call(
        _moe_up_kernel,
        grid_spec=pltpu.PrefetchScalarGridSpec(
            num_scalar_prefetch=2, grid=(F // tf, ni),
            in_specs=[pl.BlockSpec((tmf, D), lambda f, i, te, tv: (i, 0)),
                      pl.BlockSpec((None, D, tf), lambda f, i, te, tv: (e0 + te[i], 0, f)),
                      pl.BlockSpec((None, D, tf), lambda f, i, te, tv: (e0 + te[i], 0, f))],
            out_specs=pl.BlockSpec((tmf, tf), lambda f, i, te, tv: (i, f))),
        out_shape=jax.ShapeDtypeStruct((rows, F), BF16),
        compiler_params=_cparams(2),
    )(*ffn, xs, w1, w3)
    return pl.pallas_call(
        _moe_down_kernel,
        grid_spec=pltpu.PrefetchScalarGridSpec(
            num_scalar_prefetch=2, grid=(D // tn, ni),
            in_specs=[pl.BlockSpec((tmf, F), lambda j, i, te, tv: (i, 0)),
                      pl.BlockSpec((None, F, tn), lambda j, i, te, tv: (e0 + te[i], 0, j)),
                      pl.BlockSpec((tmf, LANES), lambda j, i, te, tv: (i, 0))],
            out_specs=pl.BlockSpec((tmf, tn), lambda j, i, te, tv: (i, j))),
        out_shape=jax.ShapeDtypeStruct((rows, D), BF16),
        compiler_params=_cparams(2),
    )(*ffn, a, w2, gs)


def _moe_combine_kernel(cb_ref, sl_ref, ce_ref, cs_ref, cv_ref, cf_ref, ys_ref, pos_ref, x_ref, mg_ref,
                        o_ref, *, sub):
    k = pl.program_id(0)

    @pl.when(cf_ref[k] > 0)
    def _():
        o_ref[...] = x_ref[...]

    prow = jnp.where(cv_ref[k] > 0, pos_ref[pl.ds(ce_ref[k], 1), :], -2)
    tgt = lax.broadcasted_iota(jnp.int32, (sub, prow.shape[1]), 0) + cs_ref[k] * sub
    onehot = jnp.where(prow == tgt, 1.0, 0.0).astype(BF16)
    back = lax.dot_general(onehot, ys_ref[...], (((0,), (0,)), ((), ())), preferred_element_type=F32)
    o_ref[...] += back * mg_ref[...]


def _moe_combine(lay, ys, pos_t, x, mgate, nt, comb, tb, sub):
    M = x.shape[0]
    n_exp = pos_t.shape[1]
    return pl.pallas_call(
        functools.partial(_moe_combine_kernel, sub=sub),
        grid_spec=pltpu.PrefetchScalarGridSpec(
            num_scalar_prefetch=6, grid=(nt,),
            in_specs=[pl.BlockSpec((sub, D), lambda k, cb, sl, ce, cs, cv, cf: (sl[k], 0)),
                      pl.BlockSpec((None, n_exp, tb), lambda k, cb, sl, ce, cs, cv, cf: (cb[k], 0, 0)),
                      pl.BlockSpec((tb, D), lambda k, cb, sl, ce, cs, cv, cf: (cb[k], 0)),
                      pl.BlockSpec((None, 1, D),
                                   lambda k, cb, sl, ce, cs, cv, cf: (_group_of_row(lay, cb[k] * tb), 0, 0))],
            out_specs=pl.BlockSpec((tb, D), lambda k, cb, sl, ce, cs, cv, cf: (cb[k], 0))),
        out_shape=jax.ShapeDtypeStruct((M, D), F32),
        compiler_params=_cparams(1),
    )(*comb, ys, pos_t, x, mgate)


def _moe(lay, h, x, mgate, router, w1, w3, w2, e0, n_exp, tm):
    gates = _router(h, router, min(tm, 512))
    tb = _row_tile(lay, MOE_BLOCK)
    sub = min(MOE_SUB, tb)
    tmf = max(min(MOE_FFN_ROWS, tb), sub)
    pos_t, gate_t, nt, disp, ffn, comb = _moe_plan(gates, n_exp, tb, sub, tmf // sub)
    xs, gs = _moe_dispatch(h, pos_t, gate_t, nt, disp, tb, sub)
    F = w1.shape[-1]
    tf = F // 2 if (F // 2) % LANES == 0 else F
    ys = _moe_ffn(xs, gs, w1, w3, w2, e0, ffn, tmf, tf, 1024)
    return _moe_combine(lay, ys, pos_t, x, mgate, nt, comb, tb, sub)


def kernel(x_prompt, x_sample, state_l0_gla, state_l2_hgrn2, state_l3_rwkv7, c, c_ctx, norm_mix_g, norm_ffn_g, mod_w, mod_b, gla_w_in, gla_gk_w1, gla_gk_w2, gla_gk_b, gla_norm_g, gla_w_out, hy_w_in, hy_conv_w, hy_conv_b, hy_f_w1, hy_f_b1, hy_f_w2, hy_f_b2, hy_f_w3, hy_f_freq, hy_f_bias, hy_w_out, hg_w_in, hg_lb, hg_norm_g, hg_w_out, rw_mu, rw_w_r, rw_w_k, rw_w_v, rw_w_o, rw_w0, rw_w1, rw_w2, rw_a0, rw_a1, rw_a2, rw_g1, rw_g2, rw_k_k, rw_k_a, rw_r_k, rw_ln_w, rw_ln_b, ffn_w1, ffn_w3, ffn_w2, moe_router, moe_w1, moe_w3, moe_w2, final_norm_g):
    lay = Lay(x_prompt.shape[0], x_prompt.shape[1], x_sample.shape[0], x_sample.shape[1])
    M = _M(lay)
    r0 = _R0(lay)
    tm = _row_tile(lay, 1024)
    tm_norm = _row_tile(lay, 512)
    x = jnp.concatenate([x_prompt.reshape(-1, D), x_sample.reshape(-1, D)], axis=0)
    cond = jnp.zeros((8, D), F32).at[0].set(c_ctx).at[1:1 + lay.B1].set(c)
    n_exp = moe_w1.shape[1]
    moe_w1r = moe_w1.reshape((-1,) + moe_w1.shape[2:]).astype(BF16)
    moe_w3r = moe_w3.reshape((-1,) + moe_w3.shape[2:]).astype(BF16)
    moe_w2r = moe_w2.reshape((-1,) + moe_w2.shape[2:]).astype(BF16)
    new_states = {}
    for l in range(DEPTH):
        m = _linear(cond, [mod_w], w_idx=l, tm=8, tn=1024, bias=mod_b[l], pre_silu=True)
        mods = [m[:, k * D:(k + 1) * D].reshape(8, 1, D) for k in range(6)]
        kind = l % 4
        h = _norm(x, norm_mix_g[l], lay=lay, tm=tm_norm, out_dtype=F32 if kind == 3 else BF16,
                  shift=mods[0], scale=mods[1])
        if kind == 0:
            x, new_states[0] = _gla_mixer(lay, h, x, mods[2], state_l0_gla, gla_w_in, gla_gk_w1, gla_gk_w2,
                                          gla_gk_b, gla_norm_g, gla_w_out, tm)
        elif kind == 1:
            x = _hyena_mixer(lay, h, x, mods[2], hy_w_in, hy_conv_w, hy_conv_b, hy_f_w1, hy_f_b1, hy_f_w2,
                             hy_f_b2, hy_f_w3, hy_f_freq, hy_f_bias, hy_w_out, tm)
        elif kind == 2:
            x, new_states[2] = _hgrn2_mixer(lay, h, x, mods[2], state_l2_hgrn2, l, hg_w_in, hg_lb, hg_norm_g,
                                            hg_w_out, tm)
        else:
            x, new_states[3] = _rwkv7_mixer(lay, h, x, mods[2], state_l3_rwkv7, rw_mu, rw_w_r, rw_w_k, rw_w_v,
                                            rw_w_o, rw_w0, rw_w1, rw_w2, rw_a0, rw_a1, rw_a2, rw_g1, rw_g2,
                                            rw_k_k, rw_k_a, rw_r_k, rw_ln_w, rw_ln_b, tm)
        h = _norm(x, norm_ffn_g[l], lay=lay, tm=tm_norm, out_dtype=BF16, shift=mods[3], scale=mods[4])
        j = l // 2
        if l % 2 == 0:
            a = _linear(h, [ffn_w1, ffn_w3], w_idx=j, tm=tm, tn=512, swiglu=True, out_dtype=BF16)
            x = _linear(a, [ffn_w2], w_idx=j, tm=tm, tn=512, res=x, modgate=mods[5], lay=lay)
        else:
            x = _moe(lay, h, x, mods[5], moe_router[j], moe_w1r, moe_w3r, moe_w2r, j * n_exp, n_exp, tm)
    yp = _norm(x, final_norm_g, lay=lay, tm=tm_norm, out_dtype=F32, row0=0, nrows=r0)
    ys = _norm(x, final_norm_g, lay=lay, tm=tm_norm, out_dtype=F32, row0=r0, nrows=M - r0)
    return (yp.reshape(x_prompt.shape), ys.reshape(x_sample.shape), new_states[0], new_states[2],
            new_states[3])
```

```python
import collections
import functools
import math

import jax
import jax.numpy as jnp
from jax import lax
from jax.experimental import pallas as pl
from jax.experimental.pallas import tpu as pltpu

F32 = jnp.float32
BF16 = jnp.bfloat16
HIGHEST = lax.Precision.HIGHEST

D = 2048
DEPTH = 4
CHUNK = 64
EPS = 1e-6
GRID_W = 64
GLA_HEADS, GLA_DK, GLA_DV = 4, 256, 512
GLA_GATE_RANK, GLA_GATE_NORM = 16, 16.0
HY_ORDER, HY_BANDS, HY_WIDTH = 2, 16, 64
HY_FAST, HY_SLOW, HY_TARGET = 0.3, 1.5, 1e-2
HG_HEADS, HG_DK, HG_DV = 16, 128, 128
RW_HEADS, RW_HEAD = 32, 64
RW_RANK = 96
RW_GN_EPS = 64e-5
N_EXPERTS = 8
LANES = 128
VMEM_LIMIT_MB = 56
LINEAR_VMEM_BUDGET = 44 * 1024 * 1024
HY_BLOCK_MAX = 1024
RW_SCAN_HEADS = 8
RW_SCAN_ROWS = 256
GLA_SCAN_ROWS = 512
MOE_BLOCK = 1024
MOE_SUB = 128
MOE_FFN_ROWS = 512

Lay = collections.namedtuple("Lay", "B0 L0 B1 L1")


def _R0(lay):
    return lay.B0 * lay.L0


def _M(lay):
    return lay.B0 * lay.L0 + lay.B1 * lay.L1


def _group_of_row(lay, r):
    r0 = _R0(lay)
    return jnp.where(r < r0, 0, 1 + (r - r0) // lay.L1)


def _cparams(n_axes, vmem_mb=VMEM_LIMIT_MB):
    return pltpu.CompilerParams(dimension_semantics=("arbitrary",) * n_axes,
                                vmem_limit_bytes=vmem_mb * 1024 * 1024)


def _row_tile(lay, cap):
    t = cap
    while (_R0(lay) % t) or (lay.L1 % t) or (lay.L0 % t and t % lay.L0):
        t //= 2
    return t


def _sigmoid(x):
    return 1.0 / (1.0 + jnp.exp(-x))


def _softplus(x):
    return jnp.maximum(x, 0.0) + jnp.log(1.0 + jnp.exp(-jnp.abs(x)))


def _linear_kernel(*refs, n_w, cache_w, has_bias, has_res, has_mod, has_row, row_col, act,
                   pre_silu, swiglu, hp):
    it = iter(refs)
    x_ref = next(it)
    w_refs = [next(it) for _ in range(n_w)]
    b_ref = next(it) if has_bias else None
    res_ref = next(it) if has_res else None
    mod_ref = next(it) if has_mod else None
    row_ref = next(it) if has_row else None
    o_ref = next(it)
    scr = [next(it) for _ in range(n_w)] if cache_w else None

    x = x_ref[...]
    if pre_silu:
        x = x.astype(F32)
        x = x * _sigmoid(x)
    if hp:
        accs = [jnp.dot(x.astype(F32), w[...].astype(F32), precision=HIGHEST,
                        preferred_element_type=F32) for w in w_refs]
    else:
        if cache_w:
            @pl.when(pl.program_id(1) == 0)
            def _():
                for s, w in zip(scr, w_refs):
                    s[...] = w[...].astype(BF16)
            wv = [s[...] for s in scr]
        else:
            wv = [w[...].astype(BF16) for w in w_refs]
        xb = x.astype(BF16)
        accs = [jnp.dot(xb, w, preferred_element_type=F32) for w in wv]
    if swiglu:
        a = accs[0]
        acc = a * _sigmoid(a) * accs[1]
    else:
        acc = accs[0]
    if has_bias:
        acc = acc + b_ref[...]
    if act == "tanh":
        acc = jnp.tanh(acc)
    elif act == "sigmoid":
        acc = _sigmoid(acc)
    if has_row:
        acc = acc * row_ref[:, row_col:row_col + 1]
    if has_mod:
        acc = acc * mod_ref[...]
    if has_res:
        acc = res_ref[...] + acc
    o_ref[...] = acc.astype(o_ref.dtype)


def _linear(x, ws, *, tm, tn, out_dtype=F32, w_idx=None, x_idx=None, bias=None, act=None,
            pre_silu=False, swiglu=False, res=None, modgate=None, lay=None, rowgate=None,
            rowgate_col=0, hp=False):
    n_w = len(ws)
    M, K = x.shape[-2:]
    N = ws[0].shape[-1]
    tm = min(tm, M)
    tn = min(tn, N)

    def vmem_bytes(tm_):
        cache = (not hp) and M // tm_ > 1 and ws[0].dtype != BF16
        b = 2 * tm_ * K * x.dtype.itemsize
        b += n_w * K * tn * (2 * ws[0].dtype.itemsize + (2 if cache else 0))
        b += 2 * tm_ * tn * jnp.dtype(out_dtype).itemsize + 3 * tm_ * tn * 4
        b += 2 * tm_ * tn * 4 if res is not None else 0
        b += 2 * tm_ * LANES * 4 if rowgate is not None else 0
        return b

    while vmem_bytes(tm) > LINEAR_VMEM_BUDGET and tm > 256:
        tm //= 2
    assert M % tm == 0 and N % tn == 0, (M, tm, N, tn)
    n_i = M // tm
    cache_w = (not hp) and n_i > 1 and ws[0].dtype != BF16

    if x_idx is None:
        x_spec = pl.BlockSpec((tm, K), lambda j, i: (i, 0))
    else:
        x_spec = pl.BlockSpec((None, tm, K), lambda j, i: (x_idx, i, 0))
    if w_idx is None:
        w_spec = pl.BlockSpec((K, tn), lambda j, i: (0, j))
    else:
        w_spec = pl.BlockSpec((None, K, tn), lambda j, i: (w_idx, 0, j))
    in_specs = [x_spec] + [w_spec] * n_w
    args = [x] + list(ws)
    if bias is not None:
        in_specs.append(pl.BlockSpec((1, tn), lambda j, i: (0, j)))
        args.append(bias.reshape(1, N).astype(F32))
    if res is not None:
        in_specs.append(pl.BlockSpec((tm, tn), lambda j, i: (i, j)))
        args.append(res)
    if modgate is not None:
        in_specs.append(pl.BlockSpec((None, 1, tn), lambda j, i: (_group_of_row(lay, i * tm), 0, j)))
        args.append(modgate)
    if rowgate is not None:
        in_specs.append(pl.BlockSpec((tm, LANES), lambda j, i: (i, 0)))
        args.append(rowgate)
    kern = functools.partial(
        _linear_kernel, n_w=n_w, cache_w=cache_w, has_bias=bias is not None, has_res=res is not None,
        has_mod=modgate is not None, has_row=rowgate is not None, row_col=rowgate_col, act=act,
        pre_silu=pre_silu, swiglu=swiglu, hp=hp)
    scratch = [pltpu.VMEM((K, tn), BF16) for _ in range(n_w)] if cache_w else []
    return pl.pallas_call(
        kern,
        grid=(N // tn, n_i),
        in_specs=in_specs,
        out_specs=pl.BlockSpec((tm, tn), lambda j, i: (i, j)),
        out_shape=jax.ShapeDtypeStruct((M, N), out_dtype),
        scratch_shapes=scratch,
        compiler_params=_cparams(2),
    )(*args)


def _norm_kernel(*refs, has_mod):
    if has_mod:
        x_ref, g_ref, sh_ref, sc_ref, o_ref = refs
    else:
        x_ref, g_ref, o_ref = refs
    x = x_ref[...]
    y = x * lax.rsqrt(jnp.mean(x * x, axis=-1, keepdims=True) + EPS) * g_ref[...]
    if has_mod:
        y = y * (1.0 + sc_ref[...]) + sh_ref[...]
    o_ref[...] = y.astype(o_ref.dtype)


def _norm(x, g, *, lay, tm, out_dtype, shift=None, scale=None, row0=0, nrows=None):
    nrows = x.shape[0] if nrows is None else nrows
    assert row0 % tm == 0 and nrows % tm == 0
    off = row0 // tm
    has_mod = shift is not None
    in_specs = [pl.BlockSpec((tm, D), lambda i: (i + off, 0)),
                pl.BlockSpec((1, D), lambda i: (0, 0))]
    args = [x, g.reshape(1, D)]
    if has_mod:
        mspec = pl.BlockSpec((None, 1, D), lambda i: (_group_of_row(lay, (i + off) * tm), 0, 0))
        in_specs += [mspec, mspec]
        args += [shift, scale]
    return pl.pallas_call(
        functools.partial(_norm_kernel, has_mod=has_mod),
        grid=(nrows // tm,),
        in_specs=in_specs,
        out_specs=pl.BlockSpec((tm, D), lambda i: (i, 0)),
        out_shape=jax.ShapeDtypeStruct((nrows, D), out_dtype),
        compiler_params=_cparams(1),
    )(*args)


def _scan_kernel(*refs, mode, reverse, T, Hb, dk, dv, zero_init, emit_state, nT):
    it = iter(refs)
    q_ref = next(it)
    k_ref = next(it) if mode == "gla" else None
    f_ref = next(it)
    v_ref = next(it)
    lb_ref = next(it) if mode == "hgrn" else None
    s0_ref = None if zero_init else next(it)
    o_ref = next(it)
    so_ref = next(it) if emit_state else None
    s_scr = next(it)
    C = CHUNK
    t = pl.program_id(2)

    @pl.when(t == 0)
    def _():
        for hh in range(Hb):
            if zero_init:
                s_scr[hh] = jnp.zeros((dv, dk), F32)
            else:
                s_scr[hh] = s0_ref[hh].T

    row = lax.broadcasted_iota(jnp.int32, (C, C), 0)
    col = lax.broadcasted_iota(jnp.int32, (C, C), 1)
    incl = (row <= col) if reverse else (row >= col)
    tri = incl.astype(F32)
    ref_i = (C - 1 - C // 2) if reverse else C // 2
    last_i = 0 if reverse else C - 1
    nc = T // C
    nt_dims = (((1,), (1,)), ((), ()))
    tn_dims = (((0,), (0,)), ((), ()))

    order = [(nc - 1 - i) if reverse else i for i in range(nc)]
    items = [(cc, hh) for cc in order for hh in range(Hb)]
    wide = {}
    for cc in order:
        rows = slice(cc * C, (cc + 1) * C)
        q = q_ref[rows, :]
        fx = f_ref[rows, :]
        if mode == "gla":
            k = k_ref[rows, :]
            logf = (jnp.minimum(fx, 0.0) - jnp.log(1.0 + jnp.exp(-jnp.abs(fx)))) * (1.0 / GLA_GATE_NORM)
            q = q * dk ** -0.5
        else:
            lb = lb_ref[...]
            f = lb + (1.0 - lb) * _sigmoid(fx)
            k = 1.0 - f
            logf = jnp.log(f)
            q = q * _sigmoid(q) * dk ** -0.5
        b = jnp.dot(tri, logf, precision=HIGHEST, preferred_element_type=F32)
        b_mid = b[ref_i:ref_i + 1, :]
        b_last = b[last_i:last_i + 1, :]
        wide[cc] = dict(qe=(q * jnp.exp(b - b_mid)).astype(BF16), ke=(k * jnp.exp(b_mid - b)).astype(BF16),
                        qi=(q * jnp.exp(b)).astype(BF16), ku=(k * jnp.exp(b_last - b)).astype(BF16),
                        dec=jnp.exp(b_last), v=v_ref[rows, :].astype(BF16))

    def head(cc, hh, name):
        w = dv if name == "v" else dk
        return wide[cc][name][:, hh * w:(hh + 1) * w]

    sc = [lax.dot_general(head(cc, hh, "qe"), head(cc, hh, "ke"), nt_dims, preferred_element_type=F32)
          for cc, hh in items]
    sc = [jnp.where(incl, s, 0.0).astype(BF16) for s in sc]
    o_intra = [jnp.dot(s, head(cc, hh, "v"), preferred_element_type=F32) for s, (cc, hh) in zip(sc, items)]
    upd = [lax.dot_general(head(cc, hh, "v"), head(cc, hh, "ku"), tn_dims, preferred_element_type=F32)
           for cc, hh in items]
    states = [s_scr[hh] for hh in range(Hb)]
    o_rows = [None] * nc
    for n, cc in enumerate(order):
        idx = [n * Hb + hh for hh in range(Hb)]
        o_heads = [o_intra[i] + lax.dot_general(head(cc, hh, "qi"), states[hh].astype(BF16), nt_dims,
                                                preferred_element_type=F32) for hh, i in enumerate(idx)]
        states = [states[hh] * head(cc, hh, "dec") + upd[i] for hh, i in enumerate(idx)]
        o_rows[cc] = jnp.concatenate(o_heads, axis=1) if Hb > 1 else o_heads[0]
    o_ref[...] = jnp.concatenate(o_rows, axis=0) if nc > 1 else o_rows[0]
    for hh in range(Hb):
        s_scr[hh] = states[hh]

    if emit_state:
        @pl.when(t == nT - 1)
        def _():
            for hh in range(Hb):
                so_ref[hh] = s_scr[hh].T


def _scan(mode, lay, seg, direction, proj, fsrc, lb, s0, o_prev):
    M = _M(lay)
    if mode == "gla":
        H, dk, dv, Hb = GLA_HEADS, GLA_DK, GLA_DV, 1
        q_c, k_c, f_c, v_c = 0, GLA_HEADS, direction * GLA_HEADS, 2 * GLA_HEADS * GLA_DK // GLA_DV
    else:
        H, dk, dv, Hb = HG_HEADS, HG_DK, HG_DV, 4
        nb = HG_HEADS // Hb
        q_c, k_c, f_c, v_c = 0, None, (1 + direction) * nb, 3 * nb
    if seg == 0:
        B, L, boff = lay.B0, lay.L0, 0
    else:
        B, L, boff = lay.B1, lay.L1, _R0(lay) // lay.L1
        assert _R0(lay) % lay.L1 == 0
    reverse = direction == 1
    T = min(L, GLA_SCAN_ROWS)
    nT = L // T
    zero_init = s0 is None
    emit_state = seg == 0
    W = proj.shape[1]
    pv = proj.reshape(M // L, L, W)
    fv = fsrc.reshape(M // L, L, fsrc.shape[1])

    def tt(t):
        return (nT - 1 - t) if reverse else t

    def cspec(width, c0):
        return pl.BlockSpec((None, T, width), lambda b, h, t: (b + boff, tt(t), c0 + h))

    in_specs = [cspec(Hb * dk, q_c)]
    args = [pv]
    if mode == "gla":
        in_specs.append(cspec(Hb * dk, k_c))
        args.append(pv)
    in_specs.append(cspec(Hb * dk, f_c))
    args.append(fv)
    in_specs.append(cspec(Hb * dv, v_c))
    args.append(pv)
    if mode == "hgrn":
        in_specs.append(pl.BlockSpec((None, 1, Hb * dk), lambda b, h, t: (direction, 0, h)))
        args.append(lb)
    if not zero_init:
        in_specs.append(pl.BlockSpec((None, None, Hb, dk, dv), lambda b, h, t: (b, direction, h, 0, 0)))
        args.append(s0)
    aliases = {}
    if o_prev is not None:
        in_specs.append(pl.BlockSpec(memory_space=pl.ANY))
        args.append(o_prev.reshape(M // L, L, H * dv))
        aliases = {len(args) - 1: 0}
    out_specs = [pl.BlockSpec((None, T, Hb * dv), lambda b, h, t: (b + boff, tt(t), h))]
    out_shape = [jax.ShapeDtypeStruct((M // L, L, H * dv), F32)]
    if emit_state:
        out_specs.append(pl.BlockSpec((None, Hb, dk, dv), lambda b, h, t: (b, h, 0, 0)))
        out_shape.append(jax.ShapeDtypeStruct((B, H, dk, dv), F32))

    def kern(*refs):
        refs = list(refs)
        if o_prev is not None:
            n_in = len(args)
            refs.pop(n_in - 1)
        _scan_kernel(*refs, mode=mode, reverse=reverse, T=T, Hb=Hb, dk=dk, dv=dv,
                     zero_init=zero_init, emit_state=emit_state, nT=nT)

    outs = pl.pallas_call(
        kern,
        grid=(B, H // Hb, nT),
        in_specs=in_specs,
        out_specs=out_specs,
        out_shape=out_shape,
        scratch_shapes=[pltpu.VMEM((Hb, dv, dk), F32)],
        input_output_aliases=aliases,
        compiler_params=_cparams(3),
    )(*args)
    o = outs[0].reshape(M, H * dv)
    return o, (outs[1] if emit_state else None)


def _gated_norm_kernel(of_ref, ob_ref, g_ref, gain_ref, o_ref, *, H, dv):
    gain = gain_ref[...]
    for h in range(H):
        cs = slice(h * dv, (h + 1) * dv)
        o = of_ref[:, cs] + ob_ref[:, cs]
        o = o * lax.rsqrt(jnp.mean(o * o, axis=-1, keepdims=True) + EPS) * gain
        g = g_ref[:, cs]
        o_ref[:, cs] = (o * (g * _sigmoid(g))).astype(o_ref.dtype)


def _gated_norm(o_f, o_b, proj, g_col, gain, H, dv, tm):
    M = o_f.shape[0]
    return pl.pallas_call(
        functools.partial(_gated_norm_kernel, H=H, dv=dv),
        grid=(M // tm,),
        in_specs=[pl.BlockSpec((tm, D), lambda i: (i, 0)),
                  pl.BlockSpec((tm, D), lambda i: (i, 0)),
                  pl.BlockSpec((tm, D), lambda i: (i, g_col)),
                  pl.BlockSpec((1, dv), lambda i: (0, 0))],
        out_specs=pl.BlockSpec((tm, D), lambda i: (i, 0)),
        out_shape=jax.ShapeDtypeStruct((M, D), BF16),
        compiler_params=_cparams(1),
    )(o_f, o_b, proj, gain.reshape(1, dv))


def _bidir_scan(mode, lay, proj, fsrc, lb, s0_sample):
    outs, states = [], []
    for d in range(2):
        o, st = _scan(mode, lay, 0, d, proj, fsrc, lb, None, None)
        o, _ = _scan(mode, lay, 1, d, proj, fsrc, lb, s0_sample, o)
        outs.append(o)
        states.append(st)
    return outs[0], outs[1], jnp.stack(states, axis=1)


def _gla_mixer(lay, h, x, mgate, s0, w_in, gk_w1, gk_w2, gk_b, norm_g, w_out, tm):
    kd = GLA_HEADS * GLA_DK
    proj = _linear(h, [w_in], tm=tm, tn=1024)
    w1cat = jnp.zeros((D, LANES), F32).at[:, :GLA_GATE_RANK].set(gk_w1[0])
    w1cat = w1cat.at[:, GLA_GATE_RANK:2 * GLA_GATE_RANK].set(gk_w1[1])
    w2cat = jnp.zeros((LANES, 2 * kd), F32).at[:GLA_GATE_RANK, :kd].set(gk_w2[0])
    w2cat = w2cat.at[GLA_GATE_RANK:2 * GLA_GATE_RANK, kd:].set(gk_w2[1])
    low = _linear(h, [w1cat], tm=tm, tn=LANES)
    gk = _linear(low, [w2cat], tm=tm, tn=1024, bias=gk_b.reshape(2 * kd))
    o_f, o_b, new_state = _bidir_scan("gla", lay, proj, gk, None, s0)
    y = _gated_norm(o_f, o_b, proj, 2, norm_g, GLA_HEADS, GLA_DV, min(tm, 512))
    x = _linear(y, [w_out], tm=tm, tn=1024, res=x, modgate=mgate, lay=lay)
    return x, new_state


def _hgrn2_mixer(lay, h, x, mgate, s0, layer_idx, w_in, lb_raw, norm_g, w_out, tm):
    proj = _linear(h, [w_in], tm=tm, tn=1024)
    lb = jnp.cumsum(jax.nn.softmax(lb_raw.astype(F32), axis=1), axis=1)
    lb = (lb - lb[:, :1])[:, layer_idx].reshape(2, 1, D)
    o_f, o_b, new_state = _bidir_scan("hgrn", lay, proj, proj, lb, s0)
    y = _gated_norm(o_f, o_b, proj, 4, norm_g, HG_HEADS, HG_DV, min(tm, 512))
    x = _linear(y, [w_out], tm=tm, tn=1024, res=x, modgate=mgate, lay=lay)
    return x, new_state


def _seq_pos(lay, r):
    r0 = _R0(lay)
    is_p = r < r0
    pos = jnp.where(is_p, r % lay.L0, (r - r0) % lay.L1)
    return is_p, pos, jnp.where(is_p, lay.L0, lay.L1)


def _hy_conv_kernel(x_ref, xp_ref, xn_ref, w_ref, b_ref, o_ref, *, lay, tm):
    _, pos0, seq_len = _seq_pos(lay, pl.program_id(0) * tm)
    x = x_ref[...]
    rows = lax.broadcasted_iota(jnp.int32, (tm, 1), 0)
    pos = (pos0 + rows) & (seq_len - 1)
    x_m1 = jnp.where(rows == 0, xp_ref[7:8, :], pltpu.roll(x, 1, 0))
    x_p1 = jnp.where(rows == tm - 1, xn_ref[0:1, :], pltpu.roll(x, tm - 1, 0))
    x_m1 = jnp.where(pos == 0, 0.0, x_m1)
    x_p1 = jnp.where(pos == seq_len - 1, 0.0, x_p1)
    w = w_ref[...]
    o_ref[...] = x_m1 * w[0:1] + x * w[1:2] + x_p1 * w[2:3] + b_ref[...]


def _hy_conv(lay, x, w, b, tm, tc):
    M, W = x.shape
    sub = 8
    return pl.pallas_call(
        functools.partial(_hy_conv_kernel, lay=lay, tm=tm),
        grid=(M // tm, W // tc),
        in_specs=[pl.BlockSpec((tm, tc), lambda i, j: (i, j)),
                  pl.BlockSpec((sub, tc), lambda i, j: (jnp.maximum(i * (tm // sub) - 1, 0), j)),
                  pl.BlockSpec((sub, tc), lambda i, j: (jnp.minimum((i + 1) * (tm // sub), M // sub - 1), j)),
                  pl.BlockSpec((3, tc), lambda i, j: (0, j)),
                  pl.BlockSpec((1, tc), lambda i, j: (0, j))],
        out_specs=pl.BlockSpec((tm, tc), lambda i, j: (i, j)),
        out_shape=jax.ShapeDtypeStruct((M, W), F32),
        compiler_params=_cparams(2),
    )(x, x, x, w, b.reshape(1, W))


def _hy_geom(L):
    P = min(L, HY_BLOCK_MAX)
    return P, L // P, P + LANES


def _dft_mats(P):
    N, Pp = 2 * P, P + LANES
    k = jnp.arange(Pp, dtype=jnp.int32)[:, None]
    m = jnp.arange(N, dtype=jnp.int32)[None, :]
    ang = ((k * m) % N).astype(F32) * (2.0 * math.pi / N)
    valid = k <= P
    cos = jnp.where(valid, jnp.cos(ang), 0.0)
    sin = jnp.where(valid, jnp.sin(ang), 0.0)
    fwd = jnp.concatenate([cos, -sin], axis=0)
    ck = jnp.where((k == 0) | (k == P), 1.0, 2.0) / N
    inv = jnp.concatenate([(cos * ck)[:, :P].T, (-sin * ck)[:, :P].T], axis=1)
    return fwd.astype(BF16), inv.astype(BF16)


def _hy_filter_kernel(z_ref, w1_ref, b1_ref, w2_ref, b2_ref, fr_ref, w3_ref, dl_ref, o_ref, n_ref):
    z = z_ref[...]
    fr = fr_ref[...]
    hid = jnp.sin(fr * (jnp.dot(z, w1_ref[...], precision=HIGHEST, preferred_element_type=F32) + b1_ref[...]))
    hid = jnp.sin(fr * (jnp.dot(hid, w2_ref[...], precision=HIGHEST, preferred_element_type=F32) + b2_ref[...]))
    f = jnp.dot(hid, w3_ref[...], precision=HIGHEST, preferred_element_type=F32)
    f = f * jnp.exp(-z[:, 0:1] * dl_ref[...]) * z[:, HY_FEAT_VALID:HY_FEAT_VALID + 1]
    o_ref[...] = f

    @pl.when(pl.program_id(1) == 0)
    def _():
        n_ref[...] = jnp.zeros(n_ref.shape, F32)
    n_ref[...] += jnp.sum(jnp.abs(f), axis=0, keepdims=True)


HY_FEAT_VALID = 1 + 2 * HY_BANDS


def _hy_filters(L, w1, b1, w2, b2, w3, freq, tm, tn):
    i = jnp.arange(2 * L, dtype=jnp.int32)
    p = jnp.where(i < L, L - i, i - L).astype(F32)
    t = p / L
    bands = jnp.arange(1, HY_BANDS + 1, dtype=F32)
    ang = (2.0 * math.pi / L) * p[:, None] * bands[None, :]
    feats = jnp.concatenate([t[:, None], jnp.cos(ang), jnp.sin(ang), (i > 0).astype(F32)[:, None]], axis=-1)
    z = jnp.zeros((2 * L, LANES), F32).at[:, :HY_FEAT_VALID + 1].set(feats)
    w1p = jnp.zeros((LANES, HY_WIDTH), F32).at[:HY_FEAT_VALID].set(w1)
    deltas = jnp.abs(jnp.linspace(math.log(HY_TARGET) / HY_SLOW, math.log(HY_TARGET) / HY_FAST, D, dtype=F32))
    nj = D // tn
    n_anti = L // tm
    return pl.pallas_call(
        _hy_filter_kernel,
        grid=(HY_ORDER * nj, 2 * L // tm),
        in_specs=[pl.BlockSpec((tm, LANES), lambda j, i: (i, 0)),
                  pl.BlockSpec((LANES, HY_WIDTH), lambda j, i: (0, 0)),
                  pl.BlockSpec((1, HY_WIDTH), lambda j, i: (0, 0)),
                  pl.BlockSpec((HY_WIDTH, HY_WIDTH), lambda j, i: (0, 0)),
                  pl.BlockSpec((1, HY_WIDTH), lambda j, i: (0, 0)),
                  pl.BlockSpec((1, HY_WIDTH), lambda j, i: (0, 0)),
                  pl.BlockSpec((HY_WIDTH, tn),
                               lambda j, i: (0, (j // nj) * 2 * nj + jnp.where(i < n_anti, nj, 0) + j % nj)),
                  pl.BlockSpec((1, tn), lambda j, i: (0, j % nj))],
        out_specs=[pl.BlockSpec((tm, tn), lambda j, i: (i, j)),
                   pl.BlockSpec((1, tn), lambda j, i: (0, j))],
        out_shape=[jax.ShapeDtypeStruct((2 * L, HY_ORDER * D), F32),
                   jax.ShapeDtypeStruct((1, HY_ORDER * D), F32)],
        compiler_params=_cparams(2),
    )(z, w1p, b1.reshape(1, -1), w2, b2.reshape(1, -1), freq.reshape(1, -1), w3, deltas.reshape(1, D))


def _bmm_kernel(*refs, n_pair, has_scale, has_gate):
    it = iter(refs)
    a_refs = [next(it) for _ in range(n_pair)]
    x_refs = [next(it) for _ in range(n_pair)]
    sc_ref = next(it) if has_scale else None
    if has_gate:
        g_ref, z_ref, zb_ref = next(it), next(it), next(it)
    o_ref = next(it)
    acc = None
    for a, x in zip(a_refs, x_refs):
        p = jnp.dot(a[...], x[...].astype(BF16), preferred_element_type=F32)
        acc = p if acc is None else acc + p
    if has_scale:
        acc = acc / sc_ref[...]
    if has_gate:
        acc = g_ref[...] * (acc + z_ref[...] * zb_ref[...])
    o_ref[...] = acc.astype(o_ref.dtype)


def _bmm(a_list, x_list, *, nblk, tr, tn, out_rows, out_cols, out_dtype, out_nblk=None, out_boff=0,
         colscale=None, gate=None, zin=None, zbias=None, o_prev=None):
    out_nblk = nblk if out_nblk is None else out_nblk
    in_specs, args = [], []
    for a, cb, kb in a_list:
        in_specs.append(pl.BlockSpec((tr, kb), lambda b, i, j, cb=cb: (i, cb)))
        args.append(a)
    for (x, boff, coff), (_, _, kb) in zip(x_list, a_list):
        in_specs.append(pl.BlockSpec((None, kb, tn), lambda b, i, j, boff=boff, coff=coff: (b + boff, 0, coff + j)))
        args.append(x)
    if colscale is not None:
        in_specs.append(pl.BlockSpec((1, tn), lambda b, i, j: (0, j)))
        args.append(colscale)
    if gate is not None:
        for arr, boff, coff in (gate, zin):
            in_specs.append(pl.BlockSpec((None, tr, tn),
                                         lambda b, i, j, boff=boff, coff=coff: (b + boff, i, coff + j)))
            args.append(arr)
        in_specs.append(pl.BlockSpec((1, tn), lambda b, i, j: (0, j)))
        args.append(zbias)
    aliases = {}
    n_real = len(args)
    if o_prev is not None:
        in_specs.append(pl.BlockSpec(memory_space=pl.ANY))
        args.append(o_prev)
        aliases = {n_real: 0}

    def kern(*refs):
        refs = list(refs)
        if o_prev is not None:
            refs.pop(n_real)
        _bmm_kernel(*refs, n_pair=len(a_list), has_scale=colscale is not None, has_gate=gate is not None)

    return pl.pallas_call(
        kern,
        grid=(nblk, out_rows // tr, out_cols // tn),
        in_specs=in_specs,
        out_specs=pl.BlockSpec((None, tr, tn), lambda b, i, j: (b + out_boff, i, j)),
        out_shape=jax.ShapeDtypeStruct((out_nblk, out_rows, out_cols), out_dtype),
        input_output_aliases=aliases,
        compiler_params=_cparams(3),
    )(*args)


def _hy_mac_kernel(ure_ref, uim_ref, gre_ref, gim_ref, yre_ref, yim_ref, *, nb):
    for i in range(nb):
        acc_re = acc_im = None
        for j in range(nb):
            dd = i - j + nb - 1
            ur, ui = ure_ref[j], uim_ref[j]
            gr, gi = gre_ref[dd], gim_ref[dd]
            re = gr * ur - gi * ui
            im = gr * ui + gi * ur
            acc_re = re if acc_re is None else acc_re + re
            acc_im = im if acc_im is None else acc_im + im
        yre_ref[i] = acc_re.astype(yre_ref.dtype)
        yim_ref[i] = acc_im.astype(yim_ref.dtype)


def _hy_mac(u, g, order, B, nb, Pp, tc):
    tr = 384 if Pp % 384 == 0 else (256 if Pp % 256 == 0 else Pp)
    nr = Pp // tr
    nd = 2 * nb - 1
    u4 = u.reshape(B, nb, 2 * Pp, D)
    ospec = pl.BlockSpec((None, nb, tr, tc), lambda i, j, b: (b, 0, i, j))
    oshape = jax.ShapeDtypeStruct((B, nb, Pp, D), BF16)
    yre, yim = pl.pallas_call(
        functools.partial(_hy_mac_kernel, nb=nb),
        grid=(nr, D // tc, B),
        in_specs=[pl.BlockSpec((None, nb, tr, tc), lambda i, j, b: (b, 0, i, j)),
                  pl.BlockSpec((None, nb, tr, tc), lambda i, j, b: (b, 0, i + nr, j)),
                  pl.BlockSpec((nd, tr, tc), lambda i, j, b: (0, i, order * (D // tc) + j)),
                  pl.BlockSpec((nd, tr, tc), lambda i, j, b: (0, i + nr, order * (D // tc) + j))],
        out_specs=[ospec, ospec],
        out_shape=[oshape, oshape],
        compiler_params=_cparams(3),
    )(u4, u4, g, g)
    return yre.reshape(B * nb, Pp, D), yim.reshape(B * nb, Pp, D)


def _hyena_mixer(lay, h, x, mgate, w_in, conv_w, conv_b, f_w1, f_b1, f_w2, f_b2, f_w3, f_freq, f_bias,
                 w_out, tm):
    M = _M(lay)
    proj = _linear(h, [w_in], tm=tm, tn=1024)
    assert lay.L0 & (lay.L0 - 1) == 0 and lay.L1 & (lay.L1 - 1) == 0
    cv = _hy_conv(lay, proj, conv_w, conv_b, tm, 512)
    z_all = None
    for seg in (0, 1):
        B, L, row_off = (lay.B0, lay.L0, 0) if seg == 0 else (lay.B1, lay.L1, _R0(lay))
        P, nb, Pp = _hy_geom(L)
        assert row_off % P == 0
        boff = row_off // P
        tn = D if P <= 256 else 512
        nj = D // tn
        fwd, inv = _dft_mats(P)
        fext, nrm = _hy_filters(L, f_w1, f_b1, f_w2, f_b2, f_w3, f_freq, min(L, 512), 512)
        fx = fext.reshape(2 * nb, P, HY_ORDER * D)
        g = _bmm([(fwd, 0, P), (fwd, 1, P)], [(fx, 1, 0), (fx, 0, 0)], nblk=2 * nb - 1, tr=Pp, tn=tn,
                 out_rows=2 * Pp, out_cols=HY_ORDER * D, out_dtype=F32, colscale=nrm)
        cvv = cv.reshape(M // P, P, 3 * D)
        zsrc = (cvv, boff, 2 * nj)
        for n in range(HY_ORDER):
            u = _bmm([(fwd, 0, P)], [zsrc], nblk=B * nb, tr=Pp, tn=tn, out_rows=2 * Pp, out_cols=D,
                     out_dtype=F32)
            yre, yim = _hy_mac(u, g, n, B, nb, Pp, 1024 if nb == 1 else 256)
            last = n == HY_ORDER - 1
            z = _bmm([(inv, 0, Pp), (inv, 1, Pp)], [(yre, 0, 0), (yim, 0, 0)], nblk=B * nb, tr=min(P, 512),
                     tn=tn, out_rows=P, out_cols=D, out_dtype=F32,
                     out_nblk=(M // P) if last else None, out_boff=boff if last else 0,
                     gate=(cvv, boff, n * nj), zin=zsrc, zbias=f_bias[n].reshape(1, D),
                     o_prev=None if (not last or z_all is None) else z_all.reshape(M // P, P, D))
            zsrc = (z, boff if last else 0, 0)
        z_all = z.reshape(M, D)
    return _linear(z_all, [w_out], tm=tm, tn=1024, res=x, modgate=mgate, lay=lay)


def _rw_shift_kernel(x_ref, xp_ref, xn_ref, mu_ref, o_ref, *, lay, tm):
    j = pl.program_id(1)
    is_p, pos0, _ = _seq_pos(lay, pl.program_id(0) * tm)
    x = x_ref[...]
    hp = xp_ref[...]
    hn = xn_ref[...]
    rows = lax.broadcasted_iota(jnp.int32, (tm, 1), 0)
    pos = pos0 + rows
    col = pos & (GRID_W - 1)
    x_m1 = jnp.where(rows == 0, hp[GRID_W - 1:GRID_W], pltpu.roll(x, 1, 0))
    x_p1 = jnp.where(rows == tm - 1, hn[0:1], pltpu.roll(x, tm - 1, 0))
    if tm > GRID_W:
        x_mw = jnp.concatenate([hp, x[:tm - GRID_W]], axis=0)
        x_pw = jnp.concatenate([x[GRID_W:], hn], axis=0)
    else:
        x_mw, x_pw = hp, hn
    ok_m1 = jnp.where(is_p, pos, col) != 0
    ok_p1 = jnp.where(is_p, pos - (lay.L0 - 1), col - (GRID_W - 1)) != 0
    s_m1 = jnp.where(ok_m1, x_m1, 0.0)
    s_p1 = jnp.where(ok_p1, x_p1, 0.0)
    s_mw = jnp.where(pos >= GRID_W, x_mw, 0.0)
    s_pw = jnp.where(pos < lay.L1 - GRID_W, x_pw, 0.0)
    n_m1 = jnp.where(is_p, 2, 1)
    n_1 = jnp.where(is_p, 4, 2)
    sh = jnp.where(j < n_m1, s_m1, jnp.where(j < n_1, s_p1, jnp.where(j == 2, s_mw, s_pw)))
    xx = sh - x
    mu = mu_ref[...]
    for k in range(6):
        o_ref[k] = (x + xx * mu[k:k + 1]).astype(o_ref.dtype)


def _rw_shift(lay, h, mu, tm):
    M = h.shape[0]
    tc = D // 4
    g = GRID_W
    return pl.pallas_call(
        functools.partial(_rw_shift_kernel, lay=lay, tm=tm),
        grid=(M // tm, 4),
        in_specs=[pl.BlockSpec((tm, tc), lambda i, j: (i, j)),
                  pl.BlockSpec((g, tc), lambda i, j: (jnp.maximum(i * (tm // g) - 1, 0), j)),
                  pl.BlockSpec((g, tc), lambda i, j: (jnp.minimum((i + 1) * (tm // g), M // g - 1), j)),
                  pl.BlockSpec((6, tc), lambda i, j: (0, j))],
        out_specs=pl.BlockSpec((6, tm, tc), lambda i, j: (0, i, j)),
        out_shape=jax.ShapeDtypeStruct((6, M, D), BF16),
        compiler_params=_cparams(2),
    )(h, h, h, mu)


def _head_sum(x, seg):
    hi = x.astype(BF16)
    lo = (x - hi.astype(F32)).astype(BF16)
    return (jnp.dot(hi, seg, preferred_element_type=F32) + jnp.dot(lo, seg, preferred_element_type=F32))


def _head_seg(tc):
    r = lax.broadcasted_iota(jnp.int32, (tc, tc), 0) // RW_HEAD
    c = lax.broadcasted_iota(jnp.int32, (tc, tc), 1) // RW_HEAD
    return (r == c).astype(BF16)


def _rw_scan_kernel(*refs, reverse, T, Hb, zero_init, emit_state, nT):
    it = iter(refs)
    r_ref, w_ref, k_ref, v_ref, a_ref = (next(it) for _ in range(5))
    w0_ref, a0_ref, kkp_ref, ka_ref, rk_ref = (next(it) for _ in range(5))
    s0_ref = None if zero_init else next(it)
    y_ref = next(it)
    bon_ref = next(it)
    so_ref = next(it) if emit_state else None
    s_scr = next(it)
    seg = _head_seg(Hb * RW_HEAD)
    C, Kd = CHUNK, RW_HEAD
    P2 = 2 * Kd
    npair = Hb // 2
    t = pl.program_id(2)
    zero_blk = jnp.zeros((Kd, Kd), F32)

    @pl.when(t == 0)
    def _():
        for p in range(npair):
            if zero_init:
                s_scr[p] = jnp.zeros((P2, P2), F32)
            else:
                s_scr[p] = jnp.concatenate(
                    [jnp.concatenate([s0_ref[2 * p], zero_blk], axis=1),
                     jnp.concatenate([zero_blk, s0_ref[2 * p + 1]], axis=1)], axis=0)

    row = lax.broadcasted_iota(jnp.int32, (C, C), 0)
    col = lax.broadcasted_iota(jnp.int32, (C, C), 1)
    tri = ((row <= col) if reverse else (row >= col)).astype(F32)
    row2 = lax.broadcasted_iota(jnp.int32, (P2, P2), 0)
    col2 = lax.broadcasted_iota(jnp.int32, (P2, P2), 1)
    same = (row2 // C) == (col2 // C)
    tr, tc_ = row2 % C, col2 % C
    incl2 = same & ((tr <= tc_) if reverse else (tr >= tc_))
    strict2 = same & ((tr < tc_) if reverse else (tr > tc_))
    eye2 = (row2 == col2).astype(F32)
    lane = lax.broadcasted_iota(jnp.int32, (C, P2), 1)
    h0 = lane < Kd
    last_i = 0 if reverse else C - 1
    nc = T // C
    nt_dims = (((1,), (1,)), ((), ()))
    tn_dims = (((0,), (0,)), ((), ()))

    def mm(a, b):
        return jnp.dot(a.astype(BF16), b.astype(BF16), preferred_element_type=F32)

    def stack(x):
        return jnp.concatenate([jnp.where(h0, x, 0.0), jnp.where(h0, 0.0, x)], axis=0)

    def fold(x2):
        return x2[:C] + x2[C:]

    states = [s_scr[p] for p in range(npair)]
    order = [(nc - 1 - i) if reverse else i for i in range(nc)]
    items = [(cc, p) for cc in order for p in range(npair)]
    r_all, k_all, v_all = r_ref[...], k_ref[...], v_ref[...]
    lw_all = -jnp.exp(-_softplus(-(w0_ref[...] + w_ref[...])) - 0.5)
    a_all = _sigmoid(a0_ref[...] + a_ref[...])
    kk_all = k_all * kkp_ref[...]
    kk_all = kk_all * lax.rsqrt(_head_sum(kk_all * kk_all, seg) + 1e-12)
    kd_all = k_all * (1.0 + (a_all - 1.0) * ka_ref[...])
    b_all = a_all * kk_all
    bon_ref[...] = _head_sum(r_all * kd_all * rk_ref[...], seg) * v_all
    wide = {}
    for cc in order:
        rows = slice(cc * C, (cc + 1) * C)
        r_, v_, lw = r_all[rows], v_all[rows], lw_all[rows]
        kk_, kd_, b_ = kk_all[rows], kd_all[rows], b_all[rows]
        cum = jnp.dot(tri, lw, precision=HIGHEST, preferred_element_type=F32)
        tot = cum[last_i:last_i + 1, :]
        e_neg = jnp.exp(-cum)
        e_rem = jnp.exp(tot - cum)
        wide[cc] = dict(gam=jnp.exp(tot), alpha=kk_ * jnp.exp(cum - lw),
                        rho=r_ * jnp.exp(cum), beta=b_ * e_neg, kappa=kd_ * e_neg,
                        bet2=b_ * e_rem, kap2=kd_ * e_rem, v=v_)

    def pair(cc, p, name):
        return wide[cc][name][:, p * P2:(p + 1) * P2]

    a2 = [stack(pair(cc, p, "alpha")) for cc, p in items]
    r2 = [stack(pair(cc, p, "rho")) for cc, p in items]
    v2 = [stack(pair(cc, p, "v")) for cc, p in items]
    big = [lax.dot_general(
        jnp.concatenate([a, r], axis=0).astype(BF16),
        jnp.concatenate([stack(pair(cc, p, "beta")), stack(pair(cc, p, "kappa"))], axis=0).astype(BF16),
        nt_dims, preferred_element_type=F32) for a, r, (cc, p) in zip(a2, r2, items)]
    xk = [jnp.where(strict2, -g[:P2, :P2], 0.0) for g in big]
    l_ak = [jnp.where(strict2, g[:P2, P2:], 0.0) for g in big]
    m_rb = [jnp.where(incl2, g[P2:, :P2], 0.0) for g in big]
    m_rk = [jnp.where(incl2, g[P2:, P2:], 0.0) for g in big]
    tinv = [eye2 + x for x in xk]
    for _ in range(5):
        xk = [mm(x, x) for x in xk]
        tinv = [tv + mm(tv, x) for tv, x in zip(tinv, xk)]
    lv2 = [mm(l, v) for l, v in zip(l_ak, v2)]
    au = [mm(tv, jnp.concatenate([a, lv], axis=1)) for tv, a, lv in zip(tinv, a2, lv2)]
    mau = [mm(m, x) for m, x in zip(m_rb, au)]
    mv = [mm(m, v) for m, v in zip(m_rk, v2)]
    r_t = [fold(r - ma[:, :P2]).astype(BF16) for r, ma in zip(r2, mau)]
    y0 = [fold(m - ma[:, P2:]) for m, ma in zip(mv, mau)]
    g_low, h_add = [], []
    for x, (cc, p) in zip(au, items):
        a_t, u_t = fold(x[:, :P2]), fold(x[:, P2:])
        bet2 = pair(cc, p, "bet2")
        g_low.append(jnp.where(same, lax.dot_general(a_t.astype(BF16), bet2.astype(BF16), tn_dims,
                                                     preferred_element_type=F32), 0.0).astype(BF16))
        vu = jnp.concatenate([pair(cc, p, "v"), u_t], axis=0).astype(BF16)
        kb = jnp.concatenate([pair(cc, p, "kap2"), -bet2], axis=0).astype(BF16)
        h_add.append(jnp.where(same, lax.dot_general(vu, kb, tn_dims, preferred_element_type=F32), 0.0))
    y_rows = [None] * nc
    for n, cc in enumerate(order):
        idx = [n * npair + p for p in range(npair)]
        sb = [s.astype(BF16) for s in states]
        y_pairs = [y0[i] + lax.dot_general(r_t[i], sb[p], nt_dims, preferred_element_type=F32)
                   for p, i in enumerate(idx)]
        states = [states[p] * pair(cc, p, "gam") - jnp.dot(sb[p], g_low[i], preferred_element_type=F32)
                  + h_add[i] for p, i in enumerate(idx)]
        y_rows[cc] = jnp.concatenate(y_pairs, axis=1) if npair > 1 else y_pairs[0]
    y_ref[...] = jnp.concatenate(y_rows, axis=0) if nc > 1 else y_rows[0]
    for p in range(npair):
        s_scr[p] = states[p]

    if emit_state:
        @pl.when(t == nT - 1)
        def _():
            for p in range(npair):
                so_ref[2 * p] = states[p][:Kd, :Kd]
                so_ref[2 * p + 1] = states[p][Kd:, Kd:]


def _rw_scan(lay, seg, direction, r, lw_raw, k, v, a_raw, params, s0, prev):
    M = _M(lay)
    Hb = RW_SCAN_HEADS
    H = RW_HEADS
    if seg == 0:
        B, L, boff = lay.B0, lay.L0, 0
    else:
        B, L, boff = lay.B1, lay.L1, _R0(lay) // lay.L1
    reverse = direction == 1
    T = min(L, RW_SCAN_ROWS)
    nT = L // T
    zero_init = s0 is None
    emit_state = seg == 0
    wd = Hb * RW_HEAD

    def tt(t):
        return (nT - 1 - t) if reverse else t

    spec = pl.BlockSpec((None, T, wd), lambda bb, h, t: (bb + boff, tt(t), h))
    args = [a.reshape(M // L, L, D) for a in (r, lw_raw, k, v, a_raw)]
    in_specs = [spec] * 5
    w0, a0, k_k, k_a, r_k = params
    dir_spec = pl.BlockSpec((None, 1, wd), lambda bb, h, t: (direction, 0, h))
    one_spec = pl.BlockSpec((1, wd), lambda bb, h, t: (0, h))
    in_specs += [dir_spec, dir_spec, one_spec, one_spec, one_spec]
    args += [w0.reshape(2, 1, D), a0.reshape(2, 1, D), k_k.reshape(1, D), k_a.reshape(1, D), r_k.reshape(1, D)]
    if not zero_init:
        in_specs.append(pl.BlockSpec((None, None, Hb, RW_HEAD, RW_HEAD),
                                     lambda bb, h, t: (bb, direction, h, 0, 0)))
        args.append(s0)
    aliases = {}
    n_real = len(args)
    if prev is not None:
        in_specs += [pl.BlockSpec(memory_space=pl.ANY)] * 2
        args += [p.reshape(M // L, L, D) for p in prev]
        aliases = {n_real: 0, n_real + 1: 1}
    out_specs = [spec, spec]
    out_shape = [jax.ShapeDtypeStruct((M // L, L, D), F32)] * 2
    if emit_state:
        out_specs.append(pl.BlockSpec((None, Hb, RW_HEAD, RW_HEAD), lambda bb, h, t: (bb, h, 0, 0)))
        out_shape.append(jax.ShapeDtypeStruct((B, H, RW_HEAD, RW_HEAD), F32))

    def kern(*refs):
        refs = list(refs)
        if prev is not None:
            del refs[n_real:n_real + 2]
        _rw_scan_kernel(*refs, reverse=reverse, T=T, Hb=Hb, zero_init=zero_init, emit_state=emit_state, nT=nT)

    outs = pl.pallas_call(
        kern,
        grid=(B, H // Hb, nT),
        in_specs=in_specs,
        out_specs=out_specs,
        out_shape=out_shape,
        scratch_shapes=[pltpu.VMEM((Hb // 2, 2 * RW_HEAD, 2 * RW_HEAD), F32)],
        input_output_aliases=aliases,
        compiler_params=_cparams(3),
    )(*args)
    return outs[0].reshape(M, D), outs[1].reshape(M, D), (outs[2] if emit_state else None)


def _rw_post_kernel(y0_ref, y1_ref, bon0_ref, bon1_ref, g_ref, lnw_ref, lnb_ref, o_ref, *, tc):
    seg = _head_seg(tc)
    y = y0_ref[...] + y1_ref[...]
    mean = _head_sum(y, seg) * (1.0 / RW_HEAD)
    yc = y - mean
    var = _head_sum(yc * yc, seg) * (1.0 / RW_HEAD)
    y = yc * lax.rsqrt(var + RW_GN_EPS) * lnw_ref[...] + lnb_ref[...]
    o_ref[...] = ((y + (bon0_ref[...] + bon1_ref[...])) * g_ref[...]).astype(o_ref.dtype)


def _rw_post(ys, bons, gate, ln_w, ln_b, tm, tc):
    M = gate.shape[0]
    big = pl.BlockSpec((tm, tc), lambda i, j: (i, j))
    one = pl.BlockSpec((1, tc), lambda i, j: (0, j))
    return pl.pallas_call(
        functools.partial(_rw_post_kernel, tc=tc),
        grid=(M // tm, D // tc),
        in_specs=[big] * 5 + [one, one],
        out_specs=big,
        out_shape=jax.ShapeDtypeStruct((M, D), BF16),
        compiler_params=_cparams(2),
    )(ys[0], ys[1], bons[0], bons[1], gate, ln_w.reshape(1, D), ln_b.reshape(1, D))


def _rwkv7_mixer(lay, h, x, mgate, s0, mu, w_r, w_k, w_v, w_o, w0, w1, w2, a0, a1, a2, g1, g2,
                 k_k, k_a, r_k, ln_w, ln_b, tm):
    x6 = _rw_shift(lay, h, mu, min(lay.L0, 256))
    r = _linear(x6, [w_r], x_idx=0, tm=tm, tn=1024)
    k = _linear(x6, [w_k], x_idx=2, tm=tm, tn=1024)
    v = _linear(x6, [w_v], x_idx=3, tm=tm, tn=1024)
    gh = _linear(x6, [g1], x_idx=5, tm=tm, tn=256, act="sigmoid", out_dtype=BF16)
    gate = _linear(gh, [g2], tm=tm, tn=1024)
    rk = RW_RANK

    def cat_in(w):
        out = jnp.zeros((D, 2 * LANES), F32)
        return out.at[:, :rk].set(w[0]).at[:, LANES:LANES + rk].set(w[1])

    def pad_out(w, d):
        return jnp.zeros((2 * LANES, D), F32).at[d * LANES:d * LANES + rk].set(w)

    tw = _linear(x6, [cat_in(w1)], x_idx=1, tm=tm, tn=2 * LANES, act="tanh", out_dtype=BF16)
    ta = _linear(x6, [cat_in(a1)], x_idx=4, tm=tm, tn=2 * LANES)
    lws = [_linear(tw, [pad_out(w2[d], d)], tm=tm, tn=1024) for d in range(2)]
    ars = [_linear(ta, [pad_out(a2[d], d)], tm=tm, tn=1024) for d in range(2)]
    params = (w0, a0, k_k, k_a, r_k)
    ys, bons, states = [], [], []
    for d in range(2):
        y, bon, st = _rw_scan(lay, 0, d, r, lws[d], k, v, ars[d], params, None, None)
        y, bon, _ = _rw_scan(lay, 1, d, r, lws[d], k, v, ars[d], params, s0, (y, bon))
        ys.append(y)
        bons.append(bon)
        states.append(st)
    yo = _rw_post(ys, bons, gate, ln_w, ln_b, min(tm, 512), 512)
    x = _linear(yo, [w_o], tm=tm, tn=1024, res=x, modgate=mgate, lay=lay)
    return x, jnp.stack(states, axis=1)


def _router_kernel(x_ref, w_ref, o_ref, *, n_exp):
    logits = jnp.dot(x_ref[...].astype(F32), w_ref[...], precision=HIGHEST, preferred_element_type=F32)
    lane = lax.broadcasted_iota(jnp.int32, logits.shape, 1)
    neg = -jnp.inf
    lg = jnp.where(lane < n_exp, logits, neg)
    m1 = jnp.max(lg, axis=-1, keepdims=True)
    i1 = jnp.min(jnp.where(lg == m1, lane, LANES), axis=-1, keepdims=True)
    lg2 = jnp.where(lane == i1, neg, lg)
    m2 = jnp.max(lg2, axis=-1, keepdims=True)
    i2 = jnp.min(jnp.where(lg2 == m2, lane, LANES), axis=-1, keepdims=True)
    e = jnp.exp(m2 - m1)
    p1 = 1.0 / (1.0 + e)
    o_ref[...] = jnp.where(lane == i1, p1, 0.0) + jnp.where(lane == i2, e * p1, 0.0)


def _router(h, router, tm):
    M = h.shape[0]
    n_exp = router.shape[1]
    wp = jnp.zeros((D, LANES), F32).at[:, :n_exp].set(router)
    return pl.pallas_call(
        functools.partial(_router_kernel, n_exp=n_exp),
        grid=(M // tm,),
        in_specs=[pl.BlockSpec((tm, D), lambda i: (i, 0)), pl.BlockSpec((D, LANES), lambda i: (0, 0))],
        out_specs=pl.BlockSpec((tm, LANES), lambda i: (i, 0)),
        out_shape=jax.ShapeDtypeStruct((M, LANES), F32),
        compiler_params=_cparams(1),
    )(h, wp)


def _moe_plan(gates, n_exp, tb, sub, grp):
    i32 = jnp.int32
    M = gates.shape[0]
    nblk = M // tb
    g3 = gates[:, :n_exp].reshape(nblk, tb, n_exp)
    sel = g3 > 0.0
    rank = jnp.cumsum(sel.astype(i32), axis=1) - 1
    pos_t = jnp.where(sel, rank, -1).transpose(0, 2, 1)
    gate_t = g3.transpose(0, 2, 1)
    ns = (jnp.sum(sel.astype(i32), axis=1) + sub - 1) // sub
    ns_t = ns.T
    tot_e = jnp.sum(ns_t, axis=1)
    pad_e = (tot_e + grp - 1) // grp * grp
    end_e = jnp.cumsum(pad_e)
    base_e = end_e - pad_e
    cum_eb = jnp.cumsum(ns_t, axis=1)
    off_eb = cum_eb - ns_t
    nt = (2 * M) // sub + nblk * n_exp + n_exp * (grp - 1)
    nt = (nt + grp - 1) // grp * grp
    k = jnp.arange(nt, dtype=i32)
    e_k = jnp.minimum(jnp.sum((end_e[None, :] <= k[:, None]).astype(i32), axis=1), n_exp - 1)
    local = k - base_e[e_k]
    valid = (local < tot_e[e_k]) & (k < end_e[-1])
    b_k = jnp.minimum(jnp.sum((cum_eb[e_k] <= local[:, None]).astype(i32), axis=1), nblk - 1)
    s_k = local - off_eb[e_k, b_k]
    disp = (jnp.where(valid, b_k, nblk - 1), e_k, jnp.where(valid, s_k, 0), valid.astype(i32))
    ffn = (e_k[::grp], valid[::grp].astype(i32))
    ns_f = ns.reshape(-1)
    cum_f = jnp.cumsum(ns_f)
    exc_f = cum_f - ns_f
    pair = jnp.minimum(jnp.sum((cum_f[None, :] <= k[:, None]).astype(i32), axis=1), nblk * n_exp - 1)
    cvalid = k < cum_f[-1]
    cb = jnp.where(cvalid, pair // n_exp, nblk - 1)
    ce = jnp.where(cvalid, pair % n_exp, 0)
    cs = jnp.where(cvalid, k - exc_f[pair], 0)
    slot = jnp.where(cvalid, base_e[ce] + off_eb[ce, cb] + cs, 0)
    blk_tot = jnp.sum(ns, axis=1)
    blk_start = jnp.cumsum(blk_tot) - blk_tot
    ngrp_b = (blk_tot + grp - 1) // grp
    gend = jnp.cumsum(ngrp_b)
    gstart = gend - ngrp_b
    ng = nt // grp + nblk
    g = jnp.arange(ng, dtype=i32)
    bg = jnp.minimum(jnp.sum((gend[None, :] <= g[:, None]).astype(i32), axis=1), nblk - 1)
    gvalid = g < gend[-1]
    lq = ((g - gstart[bg]) * grp)[:, None] + jnp.arange(grp, dtype=i32)[None, :]
    qvalid = gvalid[:, None] & (lq < blk_tot[bg][:, None])
    ci = jnp.clip(blk_start[bg][:, None] + lq, 0, nt - 1)
    slot_q = jnp.where(qvalid, slot[ci], slot[ci[:, :1]])
    comb = (jnp.where(gvalid, bg, nblk - 1), slot_q, jnp.where(qvalid, ce[ci], 0), jnp.where(qvalid, cs[ci], 0),
            qvalid.astype(i32), (gvalid & (g == gstart[bg])).astype(i32))
    return pos_t, gate_t, nt, ng, disp, ffn, comb


def _moe_dispatch_kernel(tb_ref, te_ref, ts_ref, tv_ref, h_ref, pos_ref, g_ref, xs_ref, gs_ref, *, sub):
    k = pl.program_id(0)
    e = te_ref[k]
    prow = jnp.where(tv_ref[k] > 0, pos_ref[pl.ds(e, 1), :], -2)
    grow = g_ref[pl.ds(e, 1), :]
    tgt = lax.broadcasted_iota(jnp.int32, (sub, prow.shape[1]), 0) + ts_ref[k] * sub
    hit = prow == tgt
    onehot = jnp.where(hit, 1.0, 0.0).astype(BF16)
    xs_ref[...] = jnp.dot(onehot, h_ref[...], preferred_element_type=F32).astype(xs_ref.dtype)
    gate = jnp.sum(jnp.where(hit, grow, 0.0), axis=1, keepdims=True)
    gs_ref[...] = jnp.broadcast_to(gate, gs_ref.shape)


def _moe_dispatch(h, pos_t, gate_t, nt, disp, tb, sub):
    n_exp = pos_t.shape[1]
    blk = lambda k, b, e, s, v: (b[k], 0)
    blk3 = lambda k, b, e, s, v: (b[k], 0, 0)
    out = lambda k, b, e, s, v: (k, 0)
    return pl.pallas_call(
        functools.partial(_moe_dispatch_kernel, sub=sub),
        grid_spec=pltpu.PrefetchScalarGridSpec(
            num_scalar_prefetch=4, grid=(nt,),
            in_specs=[pl.BlockSpec((tb, D), blk),
                      pl.BlockSpec((None, n_exp, tb), blk3),
                      pl.BlockSpec((None, n_exp, tb), blk3)],
            out_specs=[pl.BlockSpec((sub, D), out), pl.BlockSpec((sub, LANES), out)]),
        out_shape=[jax.ShapeDtypeStruct((nt * sub, D), BF16), jax.ShapeDtypeStruct((nt * sub, LANES), F32)],
        compiler_params=_cparams(1),
    )(*disp, h, pos_t, gate_t)


def _moe_up_kernel(te_ref, tv_ref, x_ref, w1_ref, w3_ref, o_ref):
    @pl.when(tv_ref[pl.program_id(1)] > 0)
    def _():
        x = x_ref[...]
        a = jnp.dot(x, w1_ref[...], preferred_element_type=F32)
        b = jnp.dot(x, w3_ref[...], preferred_element_type=F32)
        o_ref[...] = (a * _sigmoid(a) * b).astype(o_ref.dtype)


def _moe_down_kernel(te_ref, tv_ref, a_ref, w2_ref, gs_ref, o_ref):
    @pl.when(tv_ref[pl.program_id(1)] > 0)
    def _():
        y = jnp.dot(a_ref[...], w2_ref[...], preferred_element_type=F32)
        o_ref[...] = (y * gs_ref[:, 0:1]).astype(o_ref.dtype)


def _moe_ffn(xs, gs, w1, w3, w2, e0, ffn, tmf, tf, tn):
    rows = xs.shape[0]
    F = w1.shape[-1]
    ni = rows // tmf
    a = pl.pallas_call(
        _moe_up_kernel,
        grid_spec=pltpu.PrefetchScalarGridSpec(
            num_scalar_prefetch=2, grid=(F // tf, ni),
            in_specs=[pl.BlockSpec((tmf, D), lambda f, i, te, tv: (i, 0)),
                      pl.BlockSpec((None, D, tf), lambda f, i, te, tv: (e0 + te[i], 0, f)),
                      pl.BlockSpec((None, D, tf), lambda f, i, te, tv: (e0 + te[i], 0, f))],
            out_specs=pl.BlockSpec((tmf, tf), lambda f, i, te, tv: (i, f))),
        out_shape=jax.ShapeDtypeStruct((rows, F), BF16),
        compiler_params=_cparams(2),
    )(*ffn, xs, w1, w3)
    return pl.pallas_call(
        _moe_down_kernel,
        grid_spec=pltpu.PrefetchScalarGridSpec(
            num_scalar_prefetch=2, grid=(D // tn, ni),
            in_specs=[pl.BlockSpec((tmf, F), lambda j, i, te, tv: (i, 0)),
                      pl.BlockSpec((None, F, tn), lambda j, i, te, tv: (e0 + te[i], 0, j)),
                      pl.BlockSpec((tmf, LANES), lambda j, i, te, tv: (i, 0))],
            out_specs=pl.BlockSpec((tmf, tn), lambda j, i, te, tv: (i, j))),
        out_shape=jax.ShapeDtypeStruct((rows, D), BF16),
        compiler_params=_cparams(2),
    )(*ffn, a, w2, gs)


def _moe_combine_kernel(cb_ref, sl_ref, ce_ref, cs_ref, cv_ref, cf_ref, *refs, sub, grp):
    ys_refs = refs[:grp]
    pos_ref, x_ref, mg_ref, o_ref = refs[grp:]
    g = pl.program_id(0)

    @pl.when(cf_ref[g] > 0)
    def _():
        o_ref[...] = x_ref[...]

    hots = []
    for q in range(grp):
        prow = jnp.where(cv_ref[g, q] > 0, pos_ref[pl.ds(ce_ref[g, q], 1), :], -2)
        tgt = lax.broadcasted_iota(jnp.int32, (sub, prow.shape[1]), 0) + cs_ref[g, q] * sub
        hots.append(jnp.where(prow == tgt, 1.0, 0.0).astype(BF16))
    onehot = jnp.concatenate(hots, axis=0)
    ys = jnp.concatenate([r[...] for r in ys_refs], axis=0)
    back = lax.dot_general(onehot, ys, (((0,), (0,)), ((), ())), preferred_element_type=F32)
    o_ref[...] += back * mg_ref[...]


def _moe_combine(lay, ys, pos_t, x, mgate, ng, comb, tb, sub, grp):
    M = x.shape[0]
    n_exp = pos_t.shape[1]
    ys_specs = [pl.BlockSpec((sub, D), lambda g, cb, sl, ce, cs, cv, cf, q=q: (sl[g, q], 0)) for q in range(grp)]
    return pl.pallas_call(
        functools.partial(_moe_combine_kernel, sub=sub, grp=grp),
        grid_spec=pltpu.PrefetchScalarGridSpec(
            num_scalar_prefetch=6, grid=(ng,),
            in_specs=ys_specs + [
                pl.BlockSpec((None, n_exp, tb), lambda g, cb, sl, ce, cs, cv, cf: (cb[g], 0, 0)),
                pl.BlockSpec((tb, D), lambda g, cb, sl, ce, cs, cv, cf: (cb[g], 0)),
                pl.BlockSpec((None, 1, D),
                             lambda g, cb, sl, ce, cs, cv, cf: (_group_of_row(lay, cb[g] * tb), 0, 0))],
            out_specs=pl.BlockSpec((tb, D), lambda g, cb, sl, ce, cs, cv, cf: (cb[g], 0))),
        out_shape=jax.ShapeDtypeStruct((M, D), F32),
        compiler_params=_cparams(1),
    )(*comb, *([ys] * grp), pos_t, x, mgate)


def _moe(lay, h, x, mgate, router, w1, w3, w2, e0, n_exp, tm):
    gates = _router(h, router, min(tm, 512))
    tb = _row_tile(lay, MOE_BLOCK)
    sub = min(MOE_SUB, tb)
    tmf = max(min(MOE_FFN_ROWS, tb), sub)
    grp = tmf // sub
    pos_t, gate_t, nt, ng, disp, ffn, comb = _moe_plan(gates, n_exp, tb, sub, grp)
    xs, gs = _moe_dispatch(h, pos_t, gate_t, nt, disp, tb, sub)
    F = w1.shape[-1]
    tf = F // 2 if (F // 2) % LANES == 0 else F
    ys = _moe_ffn(xs, gs, w1, w3, w2, e0, ffn, tmf, tf, 1024)
    return _moe_combine(lay, ys, pos_t, x, mgate, ng, comb, tb, sub, grp)


def kernel(x_prompt, x_sample, state_l0_gla, state_l2_hgrn2, state_l3_rwkv7, c, c_ctx, norm_mix_g, norm_ffn_g, mod_w, mod_b, gla_w_in, gla_gk_w1, gla_gk_w2, gla_gk_b, gla_norm_g, gla_w_out, hy_w_in, hy_conv_w, hy_conv_b, hy_f_w1, hy_f_b1, hy_f_w2, hy_f_b2, hy_f_w3, hy_f_freq, hy_f_bias, hy_w_out, hg_w_in, hg_lb, hg_norm_g, hg_w_out, rw_mu, rw_w_r, rw_w_k, rw_w_v, rw_w_o, rw_w0, rw_w1, rw_w2, rw_a0, rw_a1, rw_a2, rw_g1, rw_g2, rw_k_k, rw_k_a, rw_r_k, rw_ln_w, rw_ln_b, ffn_w1, ffn_w3, ffn_w2, moe_router, moe_w1, moe_w3, moe_w2, final_norm_g):
    lay = Lay(x_prompt.shape[0], x_prompt.shape[1], x_sample.shape[0], x_sample.shape[1])
    M = _M(lay)
    r0 = _R0(lay)
    tm = _row_tile(lay, 1024)
    tm_norm = _row_tile(lay, 512)
    x = jnp.concatenate([x_prompt.reshape(-1, D), x_sample.reshape(-1, D)], axis=0)
    cond = jnp.zeros((8, D), F32).at[0].set(c_ctx).at[1:1 + lay.B1].set(c)
    n_exp = moe_w1.shape[1]
    moe_w1r = moe_w1.reshape((-1,) + moe_w1.shape[2:]).astype(BF16)
    moe_w3r = moe_w3.reshape((-1,) + moe_w3.shape[2:]).astype(BF16)
    moe_w2r = moe_w2.reshape((-1,) + moe_w2.shape[2:]).astype(BF16)
    new_states = {}
    for l in range(DEPTH):
        m = _linear(cond, [mod_w], w_idx=l, tm=8, tn=1024, bias=mod_b[l], pre_silu=True)
        mods = [m[:, k * D:(k + 1) * D].reshape(8, 1, D) for k in range(6)]
        kind = l % 4
        h = _norm(x, norm_mix_g[l], lay=lay, tm=tm_norm, out_dtype=F32 if kind == 3 else BF16,
                  shift=mods[0], scale=mods[1])
        if kind == 0:
            x, new_states[0] = _gla_mixer(lay, h, x, mods[2], state_l0_gla, gla_w_in, gla_gk_w1, gla_gk_w2,
                                          gla_gk_b, gla_norm_g, gla_w_out, tm)
        elif kind == 1:
            x = _hyena_mixer(lay, h, x, mods[2], hy_w_in, hy_conv_w, hy_conv_b, hy_f_w1, hy_f_b1, hy_f_w2,
                             hy_f_b2, hy_f_w3, hy_f_freq, hy_f_bias, hy_w_out, tm)
        elif kind == 2:
            x, new_states[2] = _hgrn2_mixer(lay, h, x, mods[2], state_l2_hgrn2, l, hg_w_in, hg_lb, hg_norm_g,
                                            hg_w_out, tm)
        else:
            x, new_states[3] = _rwkv7_mixer(lay, h, x, mods[2], state_l3_rwkv7, rw_mu, rw_w_r, rw_w_k, rw_w_v,
                                            rw_w_o, rw_w0, rw_w1, rw_w2, rw_a0, rw_a1, rw_a2, rw_g1, rw_g2,
                                            rw_k_k, rw_k_a, rw_r_k, rw_ln_w, rw_ln_b, tm)
        h = _norm(x, norm_ffn_g[l], lay=lay, tm=tm_norm, out_dtype=BF16, shift=mods[3], scale=mods[4])
        j = l // 2
        if l % 2 == 0:
            a = _linear(h, [ffn_w1, ffn_w3], w_idx=j, tm=tm, tn=512, swiglu=True, out_dtype=BF16)
            x = _linear(a, [ffn_w2], w_idx=j, tm=tm, tn=512, res=x, modgate=mods[5], lay=lay)
        else:
            x = _moe(lay, h, x, mods[5], moe_router[j], moe_w1r, moe_w3r, moe_w2r, j * n_exp, n_exp, tm)
    yp = _norm(x, final_norm_g, lay=lay, tm=tm_norm, out_dtype=F32, row0=0, nrows=r0)
    ys = _norm(x, final_norm_g, lay=lay, tm=tm_norm, out_dtype=F32, row0=r0, nrows=M - r0)
    return (yp.reshape(x_prompt.shape), ys.reshape(x_sample.shape), new_states[0], new_states[2],
            new_states[3])
```

```python
import collections
import functools
import math

import jax
import jax.numpy as jnp
from jax import lax
from jax.experimental import pallas as pl
from jax.experimental.pallas import tpu as pltpu

F32 = jnp.float32
BF16 = jnp.bfloat16
HIGHEST = lax.Precision.HIGHEST

D = 2048
DEPTH = 4
CHUNK = 64
EPS = 1e-6
GRID_W = 64
GLA_HEADS, GLA_DK, GLA_DV = 4, 256, 512
GLA_GATE_RANK, GLA_GATE_NORM = 16, 16.0
HY_ORDER, HY_BANDS, HY_WIDTH = 2, 16, 64
HY_FAST, HY_SLOW, HY_TARGET = 0.3, 1.5, 1e-2
HG_HEADS, HG_DK, HG_DV = 16, 128, 128
RW_HEADS, RW_HEAD = 32, 64
RW_RANK = 96
RW_GN_EPS = 64e-5
N_EXPERTS = 8
LANES = 128
VMEM_LIMIT_MB = 56
LINEAR_VMEM_BUDGET = 44 * 1024 * 1024
HY_BLOCK_MAX = 1024
RW_SCAN_HEADS = 8
RW_SCAN_ROWS = 256
GLA_SCAN_ROWS = 512
MOE_BLOCK = 1024
MOE_SUB = 128
MOE_FFN_ROWS = 512

Lay = collections.namedtuple("Lay", "B0 L0 B1 L1")


def _R0(lay):
    return lay.B0 * lay.L0


def _M(lay):
    return lay.B0 * lay.L0 + lay.B1 * lay.L1


def _group_of_row(lay, r):
    r0 = _R0(lay)
    return jnp.where(r < r0, 0, 1 + (r - r0) // lay.L1)


def _cparams(n_axes, vmem_mb=VMEM_LIMIT_MB):
    return pltpu.CompilerParams(dimension_semantics=("arbitrary",) * n_axes,
                                vmem_limit_bytes=vmem_mb * 1024 * 1024)


def _row_tile(lay, cap):
    t = cap
    while (_R0(lay) % t) or (lay.L1 % t) or (lay.L0 % t and t % lay.L0):
        t //= 2
    return t


def _sigmoid(x):
    return 1.0 / (1.0 + jnp.exp(-x))


def _softplus(x):
    return jnp.maximum(x, 0.0) + jnp.log(1.0 + jnp.exp(-jnp.abs(x)))


def _linear_kernel(*refs, n_w, cache_w, has_bias, has_res, has_mod, has_row, row_col, act,
                   pre_silu, swiglu, hp):
    it = iter(refs)
    x_ref = next(it)
    w_refs = [next(it) for _ in range(n_w)]
    b_ref = next(it) if has_bias else None
    res_ref = next(it) if has_res else None
    mod_ref = next(it) if has_mod else None
    row_ref = next(it) if has_row else None
    o_ref = next(it)
    scr = [next(it) for _ in range(n_w)] if cache_w else None

    x = x_ref[...]
    if pre_silu:
        x = x.astype(F32)
        x = x * _sigmoid(x)
    if hp:
        accs = [jnp.dot(x.astype(F32), w[...].astype(F32), precision=HIGHEST,
                        preferred_element_type=F32) for w in w_refs]
    else:
        if cache_w:
            @pl.when(pl.program_id(1) == 0)
            def _():
                for s, w in zip(scr, w_refs):
                    s[...] = w[...].astype(BF16)
            wv = [s[...] for s in scr]
        else:
            wv = [w[...].astype(BF16) for w in w_refs]
        xb = x.astype(BF16)
        accs = [jnp.dot(xb, w, preferred_element_type=F32) for w in wv]
    if swiglu:
        a = accs[0]
        acc = a * _sigmoid(a) * accs[1]
    else:
        acc = accs[0]
    if has_bias:
        acc = acc + b_ref[...]
    if act == "tanh":
        acc = jnp.tanh(acc)
    elif act == "sigmoid":
        acc = _sigmoid(acc)
    if has_row:
        acc = acc * row_ref[:, row_col:row_col + 1]
    if has_mod:
        acc = acc * mod_ref[...]
    if has_res:
        acc = res_ref[...] + acc
    o_ref[...] = acc.astype(o_ref.dtype)


def _linear(x, ws, *, tm, tn, out_dtype=F32, w_idx=None, x_idx=None, bias=None, act=None,
            pre_silu=False, swiglu=False, res=None, modgate=None, lay=None, rowgate=None,
            rowgate_col=0, hp=False):
    n_w = len(ws)
    M, K = x.shape[-2:]
    N = ws[0].shape[-1]
    tm = min(tm, M)
    tn = min(tn, N)

    def vmem_bytes(tm_):
        cache = (not hp) and M // tm_ > 1 and ws[0].dtype != BF16
        b = 2 * tm_ * K * x.dtype.itemsize
        b += n_w * K * tn * (2 * ws[0].dtype.itemsize + (2 if cache else 0))
        b += 2 * tm_ * tn * jnp.dtype(out_dtype).itemsize + 3 * tm_ * tn * 4
        b += 2 * tm_ * tn * 4 if res is not None else 0
        b += 2 * tm_ * LANES * 4 if rowgate is not None else 0
        return b

    while vmem_bytes(tm) > LINEAR_VMEM_BUDGET and tm > 256:
        tm //= 2
    assert M % tm == 0 and N % tn == 0, (M, tm, N, tn)
    n_i = M // tm
    cache_w = (not hp) and n_i > 1 and ws[0].dtype != BF16

    if x_idx is None:
        x_spec = pl.BlockSpec((tm, K), lambda j, i: (i, 0))
    else:
        x_spec = pl.BlockSpec((None, tm, K), lambda j, i: (x_idx, i, 0))
    if w_idx is None:
        w_spec = pl.BlockSpec((K, tn), lambda j, i: (0, j))
    else:
        w_spec = pl.BlockSpec((None, K, tn), lambda j, i: (w_idx, 0, j))
    in_specs = [x_spec] + [w_spec] * n_w
    args = [x] + list(ws)
    if bias is not None:
        in_specs.append(pl.BlockSpec((1, tn), lambda j, i: (0, j)))
        args.append(bias.reshape(1, N).astype(F32))
    if res is not None:
        in_specs.append(pl.BlockSpec((tm, tn), lambda j, i: (i, j)))
        args.append(res)
    if modgate is not None:
        in_specs.append(pl.BlockSpec((None, 1, tn), lambda j, i: (_group_of_row(lay, i * tm), 0, j)))
        args.append(modgate)
    if rowgate is not None:
        in_specs.append(pl.BlockSpec((tm, LANES), lambda j, i: (i, 0)))
        args.append(rowgate)
    kern = functools.partial(
        _linear_kernel, n_w=n_w, cache_w=cache_w, has_bias=bias is not None, has_res=res is not None,
        has_mod=modgate is not None, has_row=rowgate is not None, row_col=rowgate_col, act=act,
        pre_silu=pre_silu, swiglu=swiglu, hp=hp)
    scratch = [pltpu.VMEM((K, tn), BF16) for _ in range(n_w)] if cache_w else []
    return pl.pallas_call(
        kern,
        grid=(N // tn, n_i),
        in_specs=in_specs,
        out_specs=pl.BlockSpec((tm, tn), lambda j, i: (i, j)),
        out_shape=jax.ShapeDtypeStruct((M, N), out_dtype),
        scratch_shapes=scratch,
        compiler_params=_cparams(2),
    )(*args)


def _norm_kernel(*refs, has_mod):
    if has_mod:
        x_ref, g_ref, sh_ref, sc_ref, o_ref = refs
    else:
        x_ref, g_ref, o_ref = refs
    x = x_ref[...]
    y = x * lax.rsqrt(jnp.mean(x * x, axis=-1, keepdims=True) + EPS) * g_ref[...]
    if has_mod:
        y = y * (1.0 + sc_ref[...]) + sh_ref[...]
    o_ref[...] = y.astype(o_ref.dtype)


def _norm(x, g, *, lay, tm, out_dtype, shift=None, scale=None, row0=0, nrows=None):
    nrows = x.shape[0] if nrows is None else nrows
    assert row0 % tm == 0 and nrows % tm == 0
    off = row0 // tm
    has_mod = shift is not None
    in_specs = [pl.BlockSpec((tm, D), lambda i: (i + off, 0)),
                pl.BlockSpec((1, D), lambda i: (0, 0))]
    args = [x, g.reshape(1, D)]
    if has_mod:
        mspec = pl.BlockSpec((None, 1, D), lambda i: (_group_of_row(lay, (i + off) * tm), 0, 0))
        in_specs += [mspec, mspec]
        args += [shift, scale]
    return pl.pallas_call(
        functools.partial(_norm_kernel, has_mod=has_mod),
        grid=(nrows // tm,),
        in_specs=in_specs,
        out_specs=pl.BlockSpec((tm, D), lambda i: (i, 0)),
        out_shape=jax.ShapeDtypeStruct((nrows, D), out_dtype),
        compiler_params=_cparams(1),
    )(*args)


def _scan_kernel(*refs, mode, reverse, T, Hb, dk, dv, zero_init, emit_state, nT):
    it = iter(refs)
    q_ref = next(it)
    k_ref = next(it) if mode == "gla" else None
    f_ref = next(it)
    v_ref = next(it)
    lb_ref = next(it) if mode == "hgrn" else None
    s0_ref = None if zero_init else next(it)
    o_ref = next(it)
    so_ref = next(it) if emit_state else None
    s_scr = next(it)
    C = CHUNK
    t = pl.program_id(2)

    @pl.when(t == 0)
    def _():
        for hh in range(Hb):
            if zero_init:
                s_scr[hh] = jnp.zeros((dv, dk), F32)
            else:
                s_scr[hh] = s0_ref[hh].T

    row = lax.broadcasted_iota(jnp.int32, (C, C), 0)
    col = lax.broadcasted_iota(jnp.int32, (C, C), 1)
    incl = (row <= col) if reverse else (row >= col)
    tri = incl.astype(F32)
    ref_i = (C - 1 - C // 2) if reverse else C // 2
    last_i = 0 if reverse else C - 1
    nc = T // C
    nt_dims = (((1,), (1,)), ((), ()))
    tn_dims = (((0,), (0,)), ((), ()))

    order = [(nc - 1 - i) if reverse else i for i in range(nc)]
    items = [(cc, hh) for cc in order for hh in range(Hb)]
    wide = {}
    for cc in order:
        rows = slice(cc * C, (cc + 1) * C)
        q = q_ref[rows, :]
        fx = f_ref[rows, :]
        if mode == "gla":
            k = k_ref[rows, :]
            logf = (jnp.minimum(fx, 0.0) - jnp.log(1.0 + jnp.exp(-jnp.abs(fx)))) * (1.0 / GLA_GATE_NORM)
            q = q * dk ** -0.5
        else:
            lb = lb_ref[...]
            f = lb + (1.0 - lb) * _sigmoid(fx)
            k = 1.0 - f
            logf = jnp.log(f)
            q = q * _sigmoid(q) * dk ** -0.5
        b = jnp.dot(tri, logf, precision=HIGHEST, preferred_element_type=F32)
        b_mid = b[ref_i:ref_i + 1, :]
        b_last = b[last_i:last_i + 1, :]
        wide[cc] = dict(qe=(q * jnp.exp(b - b_mid)).astype(BF16), ke=(k * jnp.exp(b_mid - b)).astype(BF16),
                        qi=(q * jnp.exp(b)).astype(BF16), ku=(k * jnp.exp(b_last - b)).astype(BF16),
                        dec=jnp.exp(b_last), v=v_ref[rows, :].astype(BF16))

    def head(cc, hh, name):
        w = dv if name == "v" else dk
        return wide[cc][name][:, hh * w:(hh + 1) * w]

    sc = [lax.dot_general(head(cc, hh, "qe"), head(cc, hh, "ke"), nt_dims, preferred_element_type=F32)
          for cc, hh in items]
    sc = [jnp.where(incl, s, 0.0).astype(BF16) for s in sc]
    o_intra = [jnp.dot(s, head(cc, hh, "v"), preferred_element_type=F32) for s, (cc, hh) in zip(sc, items)]
    upd = [lax.dot_general(head(cc, hh, "v"), head(cc, hh, "ku"), tn_dims, preferred_element_type=F32)
           for cc, hh in items]
    states = [s_scr[hh] for hh in range(Hb)]
    o_rows = [None] * nc
    for n, cc in enumerate(order):
        idx = [n * Hb + hh for hh in range(Hb)]
        o_heads = [o_intra[i] + lax.dot_general(head(cc, hh, "qi"), states[hh].astype(BF16), nt_dims,
                                                preferred_element_type=F32) for hh, i in enumerate(idx)]
        states = [states[hh] * head(cc, hh, "dec") + upd[i] for hh, i in enumerate(idx)]
        o_rows[cc] = jnp.concatenate(o_heads, axis=1) if Hb > 1 else o_heads[0]
    o_ref[...] = jnp.concatenate(o_rows, axis=0) if nc > 1 else o_rows[0]
    for hh in range(Hb):
        s_scr[hh] = states[hh]

    if emit_state:
        @pl.when(t == nT - 1)
        def _():
            for hh in range(Hb):
                so_ref[hh] = s_scr[hh].T


def _scan(mode, lay, seg, direction, proj, fsrc, lb, s0, o_prev):
    M = _M(lay)
    if mode == "gla":
        H, dk, dv, Hb = GLA_HEADS, GLA_DK, GLA_DV, 1
        q_c, k_c, f_c, v_c = 0, GLA_HEADS, direction * GLA_HEADS, 2 * GLA_HEADS * GLA_DK // GLA_DV
    else:
        H, dk, dv, Hb = HG_HEADS, HG_DK, HG_DV, 4
        nb = HG_HEADS // Hb
        q_c, k_c, f_c, v_c = 0, None, (1 + direction) * nb, 3 * nb
    if seg == 0:
        B, L, boff = lay.B0, lay.L0, 0
    else:
        B, L, boff = lay.B1, lay.L1, _R0(lay) // lay.L1
        assert _R0(lay) % lay.L1 == 0
    reverse = direction == 1
    T = min(L, GLA_SCAN_ROWS)
    nT = L // T
    zero_init = s0 is None
    emit_state = seg == 0
    W = proj.shape[1]
    pv = proj.reshape(M // L, L, W)
    fv = fsrc.reshape(M // L, L, fsrc.shape[1])

    def tt(t):
        return (nT - 1 - t) if reverse else t

    def cspec(width, c0):
        return pl.BlockSpec((None, T, width), lambda b, h, t: (b + boff, tt(t), c0 + h))

    in_specs = [cspec(Hb * dk, q_c)]
    args = [pv]
    if mode == "gla":
        in_specs.append(cspec(Hb * dk, k_c))
        args.append(pv)
    in_specs.append(cspec(Hb * dk, f_c))
    args.append(fv)
    in_specs.append(cspec(Hb * dv, v_c))
    args.append(pv)
    if mode == "hgrn":
        in_specs.append(pl.BlockSpec((None, 1, Hb * dk), lambda b, h, t: (direction, 0, h)))
        args.append(lb)
    if not zero_init:
        in_specs.append(pl.BlockSpec((None, None, Hb, dk, dv), lambda b, h, t: (b, direction, h, 0, 0)))
        args.append(s0)
    aliases = {}
    if o_prev is not None:
        in_specs.append(pl.BlockSpec(memory_space=pl.ANY))
        args.append(o_prev.reshape(M // L, L, H * dv))
        aliases = {len(args) - 1: 0}
    out_specs = [pl.BlockSpec((None, T, Hb * dv), lambda b, h, t: (b + boff, tt(t), h))]
    out_shape = [jax.ShapeDtypeStruct((M // L, L, H * dv), F32)]
    if emit_state:
        out_specs.append(pl.BlockSpec((None, Hb, dk, dv), lambda b, h, t: (b, h, 0, 0)))
        out_shape.append(jax.ShapeDtypeStruct((B, H, dk, dv), F32))

    def kern(*refs):
        refs = list(refs)
        if o_prev is not None:
            n_in = len(args)
            refs.pop(n_in - 1)
        _scan_kernel(*refs, mode=mode, reverse=reverse, T=T, Hb=Hb, dk=dk, dv=dv,
                     zero_init=zero_init, emit_state=emit_state, nT=nT)

    outs = pl.pallas_call(
        kern,
        grid=(B, H // Hb, nT),
        in_specs=in_specs,
        out_specs=out_specs,
        out_shape=out_shape,
        scratch_shapes=[pltpu.VMEM((Hb, dv, dk), F32)],
        input_output_aliases=aliases,
        compiler_params=_cparams(3),
    )(*args)
    o = outs[0].reshape(M, H * dv)
    return o, (outs[1] if emit_state else None)


def _gated_norm_kernel(of_ref, ob_ref, g_ref, gain_ref, o_ref, *, H, dv):
    gain = gain_ref[...]
    for h in range(H):
        cs = slice(h * dv, (h + 1) * dv)
        o = of_ref[:, cs] + ob_ref[:, cs]
        o = o * lax.rsqrt(jnp.mean(o * o, axis=-1, keepdims=True) + EPS) * gain
        g = g_ref[:, cs]
        o_ref[:, cs] = (o * (g * _sigmoid(g))).astype(o_ref.dtype)


def _gated_norm(o_f, o_b, proj, g_col, gain, H, dv, tm):
    M = o_f.shape[0]
    return pl.pallas_call(
        functools.partial(_gated_norm_kernel, H=H, dv=dv),
        grid=(M // tm,),
        in_specs=[pl.BlockSpec((tm, D), lambda i: (i, 0)),
                  pl.BlockSpec((tm, D), lambda i: (i, 0)),
                  pl.BlockSpec((tm, D), lambda i: (i, g_col)),
                  pl.BlockSpec((1, dv), lambda i: (0, 0))],
        out_specs=pl.BlockSpec((tm, D), lambda i: (i, 0)),
        out_shape=jax.ShapeDtypeStruct((M, D), BF16),
        compiler_params=_cparams(1),
    )(o_f, o_b, proj, gain.reshape(1, dv))


def _bidir_scan(mode, lay, proj, fsrc, lb, s0_sample):
    outs, states = [], []
    for d in range(2):
        o, st = _scan(mode, lay, 0, d, proj, fsrc, lb, None, None)
        o, _ = _scan(mode, lay, 1, d, proj, fsrc, lb, s0_sample, o)
        outs.append(o)
        states.append(st)
    return outs[0], outs[1], jnp.stack(states, axis=1)


def _gla_mixer(lay, h, x, mgate, s0, w_in, gk_w1, gk_w2, gk_b, norm_g, w_out, tm):
    kd = GLA_HEADS * GLA_DK
    proj = _linear(h, [w_in], tm=tm, tn=1024)
    w1cat = jnp.zeros((D, LANES), F32).at[:, :GLA_GATE_RANK].set(gk_w1[0])
    w1cat = w1cat.at[:, GLA_GATE_RANK:2 * GLA_GATE_RANK].set(gk_w1[1])
    w2cat = jnp.zeros((LANES, 2 * kd), F32).at[:GLA_GATE_RANK, :kd].set(gk_w2[0])
    w2cat = w2cat.at[GLA_GATE_RANK:2 * GLA_GATE_RANK, kd:].set(gk_w2[1])
    low = _linear(h, [w1cat], tm=tm, tn=LANES)
    gk = _linear(low, [w2cat], tm=tm, tn=1024, bias=gk_b.reshape(2 * kd))
    o_f, o_b, new_state = _bidir_scan("gla", lay, proj, gk, None, s0)
    y = _gated_norm(o_f, o_b, proj, 2, norm_g, GLA_HEADS, GLA_DV, min(tm, 512))
    x = _linear(y, [w_out], tm=tm, tn=1024, res=x, modgate=mgate, lay=lay)
    return x, new_state


def _hgrn2_mixer(lay, h, x, mgate, s0, layer_idx, w_in, lb_raw, norm_g, w_out, tm):
    proj = _linear(h, [w_in], tm=tm, tn=1024)
    lb = jnp.cumsum(jax.nn.softmax(lb_raw.astype(F32), axis=1), axis=1)
    lb = (lb - lb[:, :1])[:, layer_idx].reshape(2, 1, D)
    o_f, o_b, new_state = _bidir_scan("hgrn", lay, proj, proj, lb, s0)
    y = _gated_norm(o_f, o_b, proj, 4, norm_g, HG_HEADS, HG_DV, min(tm, 512))
    x = _linear(y, [w_out], tm=tm, tn=1024, res=x, modgate=mgate, lay=lay)
    return x, new_state


def _seq_pos(lay, r):
    r0 = _R0(lay)
    is_p = r < r0
    pos = jnp.where(is_p, r % lay.L0, (r - r0) % lay.L1)
    return is_p, pos, jnp.where(is_p, lay.L0, lay.L1)


def _hy_conv_kernel(x_ref, xp_ref, xn_ref, w_ref, b_ref, o_ref, *, lay, tm):
    _, pos0, seq_len = _seq_pos(lay, pl.program_id(0) * tm)
    x = x_ref[...]
    rows = lax.broadcasted_iota(jnp.int32, (tm, 1), 0)
    pos = (pos0 + rows) & (seq_len - 1)
    x_m1 = jnp.where(rows == 0, xp_ref[7:8, :], pltpu.roll(x, 1, 0))
    x_p1 = jnp.where(rows == tm - 1, xn_ref[0:1, :], pltpu.roll(x, tm - 1, 0))
    x_m1 = jnp.where(pos == 0, 0.0, x_m1)
    x_p1 = jnp.where(pos == seq_len - 1, 0.0, x_p1)
    w = w_ref[...]
    o_ref[...] = x_m1 * w[0:1] + x * w[1:2] + x_p1 * w[2:3] + b_ref[...]


def _hy_conv(lay, x, w, b, tm, tc):
    M, W = x.shape
    sub = 8
    return pl.pallas_call(
        functools.partial(_hy_conv_kernel, lay=lay, tm=tm),
        grid=(M // tm, W // tc),
        in_specs=[pl.BlockSpec((tm, tc), lambda i, j: (i, j)),
                  pl.BlockSpec((sub, tc), lambda i, j: (jnp.maximum(i * (tm // sub) - 1, 0), j)),
                  pl.BlockSpec((sub, tc), lambda i, j: (jnp.minimum((i + 1) * (tm // sub), M // sub - 1), j)),
                  pl.BlockSpec((3, tc), lambda i, j: (0, j)),
                  pl.BlockSpec((1, tc), lambda i, j: (0, j))],
        out_specs=pl.BlockSpec((tm, tc), lambda i, j: (i, j)),
        out_shape=jax.ShapeDtypeStruct((M, W), F32),
        compiler_params=_cparams(2),
    )(x, x, x, w, b.reshape(1, W))


def _hy_geom(L):
    P = min(L, HY_BLOCK_MAX)
    return P, L // P, P + LANES


def _dft_mats(P):
    N, Pp = 2 * P, P + LANES
    k = jnp.arange(Pp, dtype=jnp.int32)[:, None]
    m = jnp.arange(N, dtype=jnp.int32)[None, :]
    ang = ((k * m) % N).astype(F32) * (2.0 * math.pi / N)
    valid = k <= P
    cos = jnp.where(valid, jnp.cos(ang), 0.0)
    sin = jnp.where(valid, jnp.sin(ang), 0.0)
    fwd = jnp.concatenate([cos, -sin], axis=0)
    ck = jnp.where((k == 0) | (k == P), 1.0, 2.0) / N
    inv = jnp.concatenate([(cos * ck)[:, :P].T, (-sin * ck)[:, :P].T], axis=1)
    return fwd.astype(BF16), inv.astype(BF16)


def _hy_filter_kernel(z_ref, w1_ref, b1_ref, w2_ref, b2_ref, fr_ref, w3_ref, dl_ref, o_ref, n_ref):
    z = z_ref[...]
    fr = fr_ref[...]
    hid = jnp.sin(fr * (jnp.dot(z, w1_ref[...], precision=HIGHEST, preferred_element_type=F32) + b1_ref[...]))
    hid = jnp.sin(fr * (jnp.dot(hid, w2_ref[...], precision=HIGHEST, preferred_element_type=F32) + b2_ref[...]))
    f = jnp.dot(hid, w3_ref[...], precision=HIGHEST, preferred_element_type=F32)
    f = f * jnp.exp(-z[:, 0:1] * dl_ref[...]) * z[:, HY_FEAT_VALID:HY_FEAT_VALID + 1]
    o_ref[...] = f

    @pl.when(pl.program_id(1) == 0)
    def _():
        n_ref[...] = jnp.zeros(n_ref.shape, F32)
    n_ref[...] += jnp.sum(jnp.abs(f), axis=0, keepdims=True)


HY_FEAT_VALID = 1 + 2 * HY_BANDS


def _hy_filters(L, w1, b1, w2, b2, w3, freq, tm, tn):
    i = jnp.arange(2 * L, dtype=jnp.int32)
    p = jnp.where(i < L, L - i, i - L).astype(F32)
    t = p / L
    bands = jnp.arange(1, HY_BANDS + 1, dtype=F32)
    ang = (2.0 * math.pi / L) * p[:, None] * bands[None, :]
    feats = jnp.concatenate([t[:, None], jnp.cos(ang), jnp.sin(ang), (i > 0).astype(F32)[:, None]], axis=-1)
    z = jnp.zeros((2 * L, LANES), F32).at[:, :HY_FEAT_VALID + 1].set(feats)
    w1p = jnp.zeros((LANES, HY_WIDTH), F32).at[:HY_FEAT_VALID].set(w1)
    deltas = jnp.abs(jnp.linspace(math.log(HY_TARGET) / HY_SLOW, math.log(HY_TARGET) / HY_FAST, D, dtype=F32))
    nj = D // tn
    n_anti = L // tm
    return pl.pallas_call(
        _hy_filter_kernel,
        grid=(HY_ORDER * nj, 2 * L // tm),
        in_specs=[pl.BlockSpec((tm, LANES), lambda j, i: (i, 0)),
                  pl.BlockSpec((LANES, HY_WIDTH), lambda j, i: (0, 0)),
                  pl.BlockSpec((1, HY_WIDTH), lambda j, i: (0, 0)),
                  pl.BlockSpec((HY_WIDTH, HY_WIDTH), lambda j, i: (0, 0)),
                  pl.BlockSpec((1, HY_WIDTH), lambda j, i: (0, 0)),
                  pl.BlockSpec((1, HY_WIDTH), lambda j, i: (0, 0)),
                  pl.BlockSpec((HY_WIDTH, tn),
                               lambda j, i: (0, (j // nj) * 2 * nj + jnp.where(i < n_anti, nj, 0) + j % nj)),
                  pl.BlockSpec((1, tn), lambda j, i: (0, j % nj))],
        out_specs=[pl.BlockSpec((tm, tn), lambda j, i: (i, j)),
                   pl.BlockSpec((1, tn), lambda j, i: (0, j))],
        out_shape=[jax.ShapeDtypeStruct((2 * L, HY_ORDER * D), F32),
                   jax.ShapeDtypeStruct((1, HY_ORDER * D), F32)],
        compiler_params=_cparams(2),
    )(z, w1p, b1.reshape(1, -1), w2, b2.reshape(1, -1), freq.reshape(1, -1), w3, deltas.reshape(1, D))


def _bmm_kernel(*refs, n_pair, has_scale, has_gate):
    it = iter(refs)
    a_refs = [next(it) for _ in range(n_pair)]
    x_refs = [next(it) for _ in range(n_pair)]
    sc_ref = next(it) if has_scale else None
    if has_gate:
        g_ref, z_ref, zb_ref = next(it), next(it), next(it)
    o_ref = next(it)
    acc = None
    for a, x in zip(a_refs, x_refs):
        p = jnp.dot(a[...], x[...].astype(BF16), preferred_element_type=F32)
        acc = p if acc is None else acc + p
    if has_scale:
        acc = acc / sc_ref[...]
    if has_gate:
        acc = g_ref[...] * (acc + z_ref[...] * zb_ref[...])
    o_ref[...] = acc.astype(o_ref.dtype)


def _bmm(a_list, x_list, *, nblk, tr, tn, out_rows, out_cols, out_dtype, out_nblk=None, out_boff=0,
         colscale=None, gate=None, zin=None, zbias=None, o_prev=None):
    out_nblk = nblk if out_nblk is None else out_nblk
    in_specs, args = [], []
    for a, cb, kb in a_list:
        in_specs.append(pl.BlockSpec((tr, kb), lambda i, b, j, cb=cb: (i, cb)))
        args.append(a)
    for (x, boff, coff), (_, _, kb) in zip(x_list, a_list):
        in_specs.append(pl.BlockSpec((None, kb, tn), lambda i, b, j, boff=boff, coff=coff: (b + boff, 0, coff + j)))
        args.append(x)
    if colscale is not None:
        in_specs.append(pl.BlockSpec((1, tn), lambda i, b, j: (0, j)))
        args.append(colscale)
    if gate is not None:
        for arr, boff, coff in (gate, zin):
            in_specs.append(pl.BlockSpec((None, tr, tn),
                                         lambda i, b, j, boff=boff, coff=coff: (b + boff, i, coff + j)))
            args.append(arr)
        in_specs.append(pl.BlockSpec((1, tn), lambda i, b, j: (0, j)))
        args.append(zbias)
    aliases = {}
    n_real = len(args)
    if o_prev is not None:
        in_specs.append(pl.BlockSpec(memory_space=pl.ANY))
        args.append(o_prev)
        aliases = {n_real: 0}

    def kern(*refs):
        refs = list(refs)
        if o_prev is not None:
            refs.pop(n_real)
        _bmm_kernel(*refs, n_pair=len(a_list), has_scale=colscale is not None, has_gate=gate is not None)

    return pl.pallas_call(
        kern,
        grid=(out_rows // tr, nblk, out_cols // tn),
        in_specs=in_specs,
        out_specs=pl.BlockSpec((None, tr, tn), lambda i, b, j: (b + out_boff, i, j)),
        out_shape=jax.ShapeDtypeStruct((out_nblk, out_rows, out_cols), out_dtype),
        input_output_aliases=aliases,
        compiler_params=_cparams(3),
    )(*args)


def _hy_mac_kernel(ure_ref, uim_ref, gre_ref, gim_ref, yre_ref, yim_ref, *, nb):
    for i in range(nb):
        acc_re = acc_im = None
        for j in range(nb):
            dd = i - j + nb - 1
            ur, ui = ure_ref[j], uim_ref[j]
            gr, gi = gre_ref[dd], gim_ref[dd]
            re = gr * ur - gi * ui
            im = gr * ui + gi * ur
            acc_re = re if acc_re is None else acc_re + re
            acc_im = im if acc_im is None else acc_im + im
        yre_ref[i] = acc_re.astype(yre_ref.dtype)
        yim_ref[i] = acc_im.astype(yim_ref.dtype)


def _hy_mac(u, g, order, B, nb, Pp, tc):
    tr = 384 if Pp % 384 == 0 else (256 if Pp % 256 == 0 else Pp)
    nr = Pp // tr
    nd = 2 * nb - 1
    u4 = u.reshape(B, nb, 2 * Pp, D)
    ospec = pl.BlockSpec((None, nb, tr, tc), lambda i, j, b: (b, 0, i, j))
    oshape = jax.ShapeDtypeStruct((B, nb, Pp, D), BF16)
    yre, yim = pl.pallas_call(
        functools.partial(_hy_mac_kernel, nb=nb),
        grid=(nr, D // tc, B),
        in_specs=[pl.BlockSpec((None, nb, tr, tc), lambda i, j, b: (b, 0, i, j)),
                  pl.BlockSpec((None, nb, tr, tc), lambda i, j, b: (b, 0, i + nr, j)),
                  pl.BlockSpec((nd, tr, tc), lambda i, j, b: (0, i, order * (D // tc) + j)),
                  pl.BlockSpec((nd, tr, tc), lambda i, j, b: (0, i + nr, order * (D // tc) + j))],
        out_specs=[ospec, ospec],
        out_shape=[oshape, oshape],
        compiler_params=_cparams(3),
    )(u4, u4, g, g)
    return yre.reshape(B * nb, Pp, D), yim.reshape(B * nb, Pp, D)


def _hyena_mixer(lay, h, x, mgate, w_in, conv_w, conv_b, f_w1, f_b1, f_w2, f_b2, f_w3, f_freq, f_bias,
                 w_out, tm):
    M = _M(lay)
    proj = _linear(h, [w_in], tm=tm, tn=1024)
    assert lay.L0 & (lay.L0 - 1) == 0 and lay.L1 & (lay.L1 - 1) == 0
    cv = _hy_conv(lay, proj, conv_w, conv_b, tm, 512)
    z_all = None
    for seg in (0, 1):
        B, L, row_off = (lay.B0, lay.L0, 0) if seg == 0 else (lay.B1, lay.L1, _R0(lay))
        P, nb, Pp = _hy_geom(L)
        assert row_off % P == 0
        boff = row_off // P
        tn = D if P <= 256 else 512
        nj = D // tn
        fwd, inv = _dft_mats(P)
        fext, nrm = _hy_filters(L, f_w1, f_b1, f_w2, f_b2, f_w3, f_freq, min(L, 512), 512)
        fx = fext.reshape(2 * nb, P, HY_ORDER * D)
        g = _bmm([(fwd, 0, P), (fwd, 1, P)], [(fx, 1, 0), (fx, 0, 0)], nblk=2 * nb - 1, tr=Pp, tn=tn,
                 out_rows=2 * Pp, out_cols=HY_ORDER * D, out_dtype=F32, colscale=nrm)
        cvv = cv.reshape(M // P, P, 3 * D)
        zsrc = (cvv, boff, 2 * nj)
        for n in range(HY_ORDER):
            u = _bmm([(fwd, 0, P)], [zsrc], nblk=B * nb, tr=Pp, tn=tn, out_rows=2 * Pp, out_cols=D,
                     out_dtype=F32)
            yre, yim = _hy_mac(u, g, n, B, nb, Pp, 1024 if nb == 1 else 256)
            last = n == HY_ORDER - 1
            z = _bmm([(inv, 0, Pp), (inv, 1, Pp)], [(yre, 0, 0), (yim, 0, 0)], nblk=B * nb, tr=min(P, 512),
                     tn=tn, out_rows=P, out_cols=D, out_dtype=F32,
                     out_nblk=(M // P) if last else None, out_boff=boff if last else 0,
                     gate=(cvv, boff, n * nj), zin=zsrc, zbias=f_bias[n].reshape(1, D),
                     o_prev=None if (not last or z_all is None) else z_all.reshape(M // P, P, D))
            zsrc = (z, boff if last else 0, 0)
        z_all = z.reshape(M, D)
    return _linear(z_all, [w_out], tm=tm, tn=1024, res=x, modgate=mgate, lay=lay)


def _rw_shift_kernel(x_ref, xp_ref, xn_ref, mu_ref, o_ref, *, lay, tm):
    j = pl.program_id(1)
    is_p, pos0, _ = _seq_pos(lay, pl.program_id(0) * tm)
    x = x_ref[...]
    hp = xp_ref[...]
    hn = xn_ref[...]
    rows = lax.broadcasted_iota(jnp.int32, (tm, 1), 0)
    pos = pos0 + rows
    col = pos & (GRID_W - 1)
    x_m1 = jnp.where(rows == 0, hp[GRID_W - 1:GRID_W], pltpu.roll(x, 1, 0))
    x_p1 = jnp.where(rows == tm - 1, hn[0:1], pltpu.roll(x, tm - 1, 0))
    if tm > GRID_W:
        x_mw = jnp.concatenate([hp, x[:tm - GRID_W]], axis=0)
        x_pw = jnp.concatenate([x[GRID_W:], hn], axis=0)
    else:
        x_mw, x_pw = hp, hn
    ok_m1 = jnp.where(is_p, pos, col) != 0
    ok_p1 = jnp.where(is_p, pos - (lay.L0 - 1), col - (GRID_W - 1)) != 0
    s_m1 = jnp.where(ok_m1, x_m1, 0.0)
    s_p1 = jnp.where(ok_p1, x_p1, 0.0)
    s_mw = jnp.where(pos >= GRID_W, x_mw, 0.0)
    s_pw = jnp.where(pos < lay.L1 - GRID_W, x_pw, 0.0)
    n_m1 = jnp.where(is_p, 2, 1)
    n_1 = jnp.where(is_p, 4, 2)
    sh = jnp.where(j < n_m1, s_m1, jnp.where(j < n_1, s_p1, jnp.where(j == 2, s_mw, s_pw)))
    xx = sh - x
    mu = mu_ref[...]
    for k in range(6):
        o_ref[k] = (x + xx * mu[k:k + 1]).astype(o_ref.dtype)


def _rw_shift(lay, h, mu, tm):
    M = h.shape[0]
    tc = D // 4
    g = GRID_W
    return pl.pallas_call(
        functools.partial(_rw_shift_kernel, lay=lay, tm=tm),
        grid=(M // tm, 4),
        in_specs=[pl.BlockSpec((tm, tc), lambda i, j: (i, j)),
                  pl.BlockSpec((g, tc), lambda i, j: (jnp.maximum(i * (tm // g) - 1, 0), j)),
                  pl.BlockSpec((g, tc), lambda i, j: (jnp.minimum((i + 1) * (tm // g), M // g - 1), j)),
                  pl.BlockSpec((6, tc), lambda i, j: (0, j))],
        out_specs=pl.BlockSpec((6, tm, tc), lambda i, j: (0, i, j)),
        out_shape=jax.ShapeDtypeStruct((6, M, D), BF16),
        compiler_params=_cparams(2),
    )(h, h, h, mu)


def _head_sum(x, seg):
    hi = x.astype(BF16)
    lo = (x - hi.astype(F32)).astype(BF16)
    return (jnp.dot(hi, seg, preferred_element_type=F32) + jnp.dot(lo, seg, preferred_element_type=F32))


def _head_seg(tc):
    r = lax.broadcasted_iota(jnp.int32, (tc, tc), 0) // RW_HEAD
    c = lax.broadcasted_iota(jnp.int32, (tc, tc), 1) // RW_HEAD
    return (r == c).astype(BF16)


def _rw_scan_kernel(*refs, reverse, T, Hb, zero_init, emit_state, nT):
    it = iter(refs)
    r_ref, w_ref, k_ref, v_ref, a_ref = (next(it) for _ in range(5))
    w0_ref, a0_ref, kkp_ref, ka_ref, rk_ref = (next(it) for _ in range(5))
    s0_ref = None if zero_init else next(it)
    y_ref = next(it)
    bon_ref = next(it)
    so_ref = next(it) if emit_state else None
    s_scr = next(it)
    seg = _head_seg(Hb * RW_HEAD)
    C, Kd = CHUNK, RW_HEAD
    P2 = 2 * Kd
    npair = Hb // 2
    t = pl.program_id(2)
    zero_blk = jnp.zeros((Kd, Kd), F32)

    @pl.when(t == 0)
    def _():
        for p in range(npair):
            if zero_init:
                s_scr[p] = jnp.zeros((P2, P2), F32)
            else:
                s_scr[p] = jnp.concatenate(
                    [jnp.concatenate([s0_ref[2 * p], zero_blk], axis=1),
                     jnp.concatenate([zero_blk, s0_ref[2 * p + 1]], axis=1)], axis=0)

    row = lax.broadcasted_iota(jnp.int32, (C, C), 0)
    col = lax.broadcasted_iota(jnp.int32, (C, C), 1)
    tri = ((row <= col) if reverse else (row >= col)).astype(F32)
    row2 = lax.broadcasted_iota(jnp.int32, (P2, P2), 0)
    col2 = lax.broadcasted_iota(jnp.int32, (P2, P2), 1)
    same = (row2 // C) == (col2 // C)
    tr, tc_ = row2 % C, col2 % C
    incl2 = same & ((tr <= tc_) if reverse else (tr >= tc_))
    strict2 = same & ((tr < tc_) if reverse else (tr > tc_))
    eye2 = (row2 == col2).astype(F32)
    lane = lax.broadcasted_iota(jnp.int32, (C, P2), 1)
    h0 = lane < Kd
    last_i = 0 if reverse else C - 1
    nc = T // C
    nt_dims = (((1,), (1,)), ((), ()))
    tn_dims = (((0,), (0,)), ((), ()))

    def mm(a, b):
        return jnp.dot(a.astype(BF16), b.astype(BF16), preferred_element_type=F32)

    def stack(x):
        return jnp.concatenate([jnp.where(h0, x, 0.0), jnp.where(h0, 0.0, x)], axis=0)

    def fold(x2):
        return x2[:C] + x2[C:]

    states = [s_scr[p] for p in range(npair)]
    order = [(nc - 1 - i) if reverse else i for i in range(nc)]
    items = [(cc, p) for cc in order for p in range(npair)]
    r_all, k_all, v_all = r_ref[...], k_ref[...], v_ref[...]
    lw_all = -jnp.exp(-_softplus(-(w0_ref[...] + w_ref[...])) - 0.5)
    a_all = _sigmoid(a0_ref[...] + a_ref[...])
    kk_all = k_all * kkp_ref[...]
    kk_all = kk_all * lax.rsqrt(_head_sum(kk_all * kk_all, seg) + 1e-12)
    kd_all = k_all * (1.0 + (a_all - 1.0) * ka_ref[...])
    b_all = a_all * kk_all
    bon_ref[...] = _head_sum(r_all * kd_all * rk_ref[...], seg) * v_all
    wide = {}
    for cc in order:
        rows = slice(cc * C, (cc + 1) * C)
        r_, v_, lw = r_all[rows], v_all[rows], lw_all[rows]
        kk_, kd_, b_ = kk_all[rows], kd_all[rows], b_all[rows]
        cum = jnp.dot(tri, lw, precision=HIGHEST, preferred_element_type=F32)
        tot = cum[last_i:last_i + 1, :]
        e_neg = jnp.exp(-cum)
        e_rem = jnp.exp(tot - cum)
        wide[cc] = dict(gam=jnp.exp(tot), alpha=kk_ * jnp.exp(cum - lw),
                        rho=r_ * jnp.exp(cum), beta=b_ * e_neg, kappa=kd_ * e_neg,
                        bet2=b_ * e_rem, kap2=kd_ * e_rem, v=v_)

    def pair(cc, p, name):
        return wide[cc][name][:, p * P2:(p + 1) * P2]

    a2 = [stack(pair(cc, p, "alpha")) for cc, p in items]
    r2 = [stack(pair(cc, p, "rho")) for cc, p in items]
    v2 = [stack(pair(cc, p, "v")) for cc, p in items]
    big = [lax.dot_general(
        jnp.concatenate([a, r], axis=0).astype(BF16),
        jnp.concatenate([stack(pair(cc, p, "beta")), stack(pair(cc, p, "kappa"))], axis=0).astype(BF16),
        nt_dims, preferred_element_type=F32) for a, r, (cc, p) in zip(a2, r2, items)]
    xk = [jnp.where(strict2, -g[:P2, :P2], 0.0) for g in big]
    l_ak = [jnp.where(strict2, g[:P2, P2:], 0.0) for g in big]
    m_rb = [jnp.where(incl2, g[P2:, :P2], 0.0) for g in big]
    m_rk = [jnp.where(incl2, g[P2:, P2:], 0.0) for g in big]
    tinv = [eye2 + x for x in xk]
    for _ in range(5):
        xk = [mm(x, x) for x in xk]
        tinv = [tv + mm(tv, x) for tv, x in zip(tinv, xk)]
    lv2 = [mm(l, v) for l, v in zip(l_ak, v2)]
    au = [mm(tv, jnp.concatenate([a, lv], axis=1)) for tv, a, lv in zip(tinv, a2, lv2)]
    mau = [mm(m, x) for m, x in zip(m_rb, au)]
    mv = [mm(m, v) for m, v in zip(m_rk, v2)]
    r_t = [fold(r - ma[:, :P2]).astype(BF16) for r, ma in zip(r2, mau)]
    y0 = [fold(m - ma[:, P2:]) for m, ma in zip(mv, mau)]
    g_low, h_add = [], []
    for x, (cc, p) in zip(au, items):
        a_t, u_t = fold(x[:, :P2]), fold(x[:, P2:])
        bet2 = pair(cc, p, "bet2")
        g_low.append(jnp.where(same, lax.dot_general(a_t.astype(BF16), bet2.astype(BF16), tn_dims,
                                                     preferred_element_type=F32), 0.0).astype(BF16))
        vu = jnp.concatenate([pair(cc, p, "v"), u_t], axis=0).astype(BF16)
        kb = jnp.concatenate([pair(cc, p, "kap2"), -bet2], axis=0).astype(BF16)
        h_add.append(jnp.where(same, lax.dot_general(vu, kb, tn_dims, preferred_element_type=F32), 0.0))
    y_rows = [None] * nc
    for n, cc in enumerate(order):
        idx = [n * npair + p for p in range(npair)]
        sb = [s.astype(BF16) for s in states]
        y_pairs = [y0[i] + lax.dot_general(r_t[i], sb[p], nt_dims, preferred_element_type=F32)
                   for p, i in enumerate(idx)]
        states = [states[p] * pair(cc, p, "gam") - jnp.dot(sb[p], g_low[i], preferred_element_type=F32)
                  + h_add[i] for p, i in enumerate(idx)]
        y_rows[cc] = jnp.concatenate(y_pairs, axis=1) if npair > 1 else y_pairs[0]
    y_ref[...] = jnp.concatenate(y_rows, axis=0) if nc > 1 else y_rows[0]
    for p in range(npair):
        s_scr[p] = states[p]

    if emit_state:
        @pl.when(t == nT - 1)
        def _():
            for p in range(npair):
                so_ref[2 * p] = states[p][:Kd, :Kd]
                so_ref[2 * p + 1] = states[p][Kd:, Kd:]


def _rw_scan(lay, seg, direction, r, lw_raw, k, v, a_raw, params, s0, prev):
    M = _M(lay)
    Hb = RW_SCAN_HEADS
    H = RW_HEADS
    if seg == 0:
        B, L, boff = lay.B0, lay.L0, 0
    else:
        B, L, boff = lay.B1, lay.L1, _R0(lay) // lay.L1
    reverse = direction == 1
    T = min(L, RW_SCAN_ROWS)
    nT = L // T
    zero_init = s0 is None
    emit_state = seg == 0
    wd = Hb * RW_HEAD

    def tt(t):
        return (nT - 1 - t) if reverse else t

    spec = pl.BlockSpec((None, T, wd), lambda bb, h, t: (bb + boff, tt(t), h))
    args = [a.reshape(M // L, L, D) for a in (r, lw_raw, k, v, a_raw)]
    in_specs = [spec] * 5
    w0, a0, k_k, k_a, r_k = params
    dir_spec = pl.BlockSpec((None, 1, wd), lambda bb, h, t: (direction, 0, h))
    one_spec = pl.BlockSpec((1, wd), lambda bb, h, t: (0, h))
    in_specs += [dir_spec, dir_spec, one_spec, one_spec, one_spec]
    args += [w0.reshape(2, 1, D), a0.reshape(2, 1, D), k_k.reshape(1, D), k_a.reshape(1, D), r_k.reshape(1, D)]
    if not zero_init:
        in_specs.append(pl.BlockSpec((None, None, Hb, RW_HEAD, RW_HEAD),
                                     lambda bb, h, t: (bb, direction, h, 0, 0)))
        args.append(s0)
    aliases = {}
    n_real = len(args)
    if prev is not None:
        in_specs += [pl.BlockSpec(memory_space=pl.ANY)] * 2
        args += [p.reshape(M // L, L, D) for p in prev]
        aliases = {n_real: 0, n_real + 1: 1}
    out_specs = [spec, spec]
    out_shape = [jax.ShapeDtypeStruct((M // L, L, D), F32)] * 2
    if emit_state:
        out_specs.append(pl.BlockSpec((None, Hb, RW_HEAD, RW_HEAD), lambda bb, h, t: (bb, h, 0, 0)))
        out_shape.append(jax.ShapeDtypeStruct((B, H, RW_HEAD, RW_HEAD), F32))

    def kern(*refs):
        refs = list(refs)
        if prev is not None:
            del refs[n_real:n_real + 2]
        _rw_scan_kernel(*refs, reverse=reverse, T=T, Hb=Hb, zero_init=zero_init, emit_state=emit_state, nT=nT)

    outs = pl.pallas_call(
        kern,
        grid=(B, H // Hb, nT),
        in_specs=in_specs,
        out_specs=out_specs,
        out_shape=out_shape,
        scratch_shapes=[pltpu.VMEM((Hb // 2, 2 * RW_HEAD, 2 * RW_HEAD), F32)],
        input_output_aliases=aliases,
        compiler_params=_cparams(3),
    )(*args)
    return outs[0].reshape(M, D), outs[1].reshape(M, D), (outs[2] if emit_state else None)


def _rw_post_kernel(y0_ref, y1_ref, bon0_ref, bon1_ref, g_ref, lnw_ref, lnb_ref, o_ref, *, tc):
    seg = _head_seg(tc)
    y = y0_ref[...] + y1_ref[...]
    mean = _head_sum(y, seg) * (1.0 / RW_HEAD)
    yc = y - mean
    var = _head_sum(yc * yc, seg) * (1.0 / RW_HEAD)
    y = yc * lax.rsqrt(var + RW_GN_EPS) * lnw_ref[...] + lnb_ref[...]
    o_ref[...] = ((y + (bon0_ref[...] + bon1_ref[...])) * g_ref[...]).astype(o_ref.dtype)


def _rw_post(ys, bons, gate, ln_w, ln_b, tm, tc):
    M = gate.shape[0]
    big = pl.BlockSpec((tm, tc), lambda i, j: (i, j))
    one = pl.BlockSpec((1, tc), lambda i, j: (0, j))
    return pl.pallas_call(
        functools.partial(_rw_post_kernel, tc=tc),
        grid=(M // tm, D // tc),
        in_specs=[big] * 5 + [one, one],
        out_specs=big,
        out_shape=jax.ShapeDtypeStruct((M, D), BF16),
        compiler_params=_cparams(2),
    )(ys[0], ys[1], bons[0], bons[1], gate, ln_w.reshape(1, D), ln_b.reshape(1, D))


def _rwkv7_mixer(lay, h, x, mgate, s0, mu, w_r, w_k, w_v, w_o, w0, w1, w2, a0, a1, a2, g1, g2,
                 k_k, k_a, r_k, ln_w, ln_b, tm):
    x6 = _rw_shift(lay, h, mu, min(lay.L0, 256))
    r = _linear(x6, [w_r], x_idx=0, tm=tm, tn=1024)
    k = _linear(x6, [w_k], x_idx=2, tm=tm, tn=1024)
    v = _linear(x6, [w_v], x_idx=3, tm=tm, tn=1024)
    gh = _linear(x6, [g1], x_idx=5, tm=tm, tn=256, act="sigmoid", out_dtype=BF16)
    gate = _linear(gh, [g2], tm=tm, tn=1024)
    rk = RW_RANK

    def cat_in(w):
        out = jnp.zeros((D, 2 * LANES), F32)
        return out.at[:, :rk].set(w[0]).at[:, LANES:LANES + rk].set(w[1])

    def pad_out(w, d):
        return jnp.zeros((2 * LANES, D), F32).at[d * LANES:d * LANES + rk].set(w)

    tw = _linear(x6, [cat_in(w1)], x_idx=1, tm=tm, tn=2 * LANES, act="tanh", out_dtype=BF16)
    ta = _linear(x6, [cat_in(a1)], x_idx=4, tm=tm, tn=2 * LANES)
    lws = [_linear(tw, [pad_out(w2[d], d)], tm=tm, tn=1024) for d in range(2)]
    ars = [_linear(ta, [pad_out(a2[d], d)], tm=tm, tn=1024) for d in range(2)]
    params = (w0, a0, k_k, k_a, r_k)
    ys, bons, states = [], [], []
    for d in range(2):
        y, bon, st = _rw_scan(lay, 0, d, r, lws[d], k, v, ars[d], params, None, None)
        y, bon, _ = _rw_scan(lay, 1, d, r, lws[d], k, v, ars[d], params, s0, (y, bon))
        ys.append(y)
        bons.append(bon)
        states.append(st)
    yo = _rw_post(ys, bons, gate, ln_w, ln_b, min(tm, 512), 512)
    x = _linear(yo, [w_o], tm=tm, tn=1024, res=x, modgate=mgate, lay=lay)
    return x, jnp.stack(states, axis=1)


def _router_kernel(x_ref, w_ref, o_ref, *, n_exp):
    logits = jnp.dot(x_ref[...].astype(F32), w_ref[...], precision=HIGHEST, preferred_element_type=F32)
    lane = lax.broadcasted_iota(jnp.int32, logits.shape, 1)
    neg = -jnp.inf
    lg = jnp.where(lane < n_exp, logits, neg)
    m1 = jnp.max(lg, axis=-1, keepdims=True)
    i1 = jnp.min(jnp.where(lg == m1, lane, LANES), axis=-1, keepdims=True)
    lg2 = jnp.where(lane == i1, neg, lg)
    m2 = jnp.max(lg2, axis=-1, keepdims=True)
    i2 = jnp.min(jnp.where(lg2 == m2, lane, LANES), axis=-1, keepdims=True)
    e = jnp.exp(m2 - m1)
    p1 = 1.0 / (1.0 + e)
    o_ref[...] = jnp.where(lane == i1, p1, 0.0) + jnp.where(lane == i2, e * p1, 0.0)


def _router(h, router, tm):
    M = h.shape[0]
    n_exp = router.shape[1]
    wp = jnp.zeros((D, LANES), F32).at[:, :n_exp].set(router)
    return pl.pallas_call(
        functools.partial(_router_kernel, n_exp=n_exp),
        grid=(M // tm,),
        in_specs=[pl.BlockSpec((tm, D), lambda i: (i, 0)), pl.BlockSpec((D, LANES), lambda i: (0, 0))],
        out_specs=pl.BlockSpec((tm, LANES), lambda i: (i, 0)),
        out_shape=jax.ShapeDtypeStruct((M, LANES), F32),
        compiler_params=_cparams(1),
    )(h, wp)


def _moe_plan(gates, n_exp, tb, sub, grp):
    i32 = jnp.int32
    M = gates.shape[0]
    nblk = M // tb
    g3 = gates[:, :n_exp].reshape(nblk, tb, n_exp)
    sel = g3 > 0.0
    rank = jnp.cumsum(sel.astype(i32), axis=1) - 1
    pos_t = jnp.where(sel, rank, -1).transpose(0, 2, 1)
    gate_t = g3.transpose(0, 2, 1)
    ns = (jnp.sum(sel.astype(i32), axis=1) + sub - 1) // sub
    ns_t = ns.T
    tot_e = jnp.sum(ns_t, axis=1)
    pad_e = (tot_e + grp - 1) // grp * grp
    end_e = jnp.cumsum(pad_e)
    base_e = end_e - pad_e
    cum_eb = jnp.cumsum(ns_t, axis=1)
    off_eb = cum_eb - ns_t
    nt = (2 * M) // sub + nblk * n_exp + n_exp * (grp - 1)
    nt = (nt + grp - 1) // grp * grp
    k = jnp.arange(nt, dtype=i32)
    e_k = jnp.minimum(jnp.sum((end_e[None, :] <= k[:, None]).astype(i32), axis=1), n_exp - 1)
    local = k - base_e[e_k]
    valid = (local < tot_e[e_k]) & (k < end_e[-1])
    ffn = (e_k[::grp], valid[::grp].astype(i32))
    ns_f = ns.reshape(-1)
    cum_f = jnp.cumsum(ns_f)
    exc_f = cum_f - ns_f
    pair = jnp.minimum(jnp.sum((cum_f[None, :] <= k[:, None]).astype(i32), axis=1), nblk * n_exp - 1)
    cvalid = k < cum_f[-1]
    cb = jnp.where(cvalid, pair // n_exp, nblk - 1)
    ce = jnp.where(cvalid, pair % n_exp, 0)
    cs = jnp.where(cvalid, k - exc_f[pair], 0)
    slot = jnp.where(cvalid, base_e[ce] + off_eb[ce, cb] + cs, 0)
    used = jnp.zeros((nt,), i32).at[jnp.where(cvalid, slot, nt)].set(1, mode="drop")
    free = jnp.argsort(used, stable=True)
    out_slot = jnp.where(cvalid, slot, free[jnp.clip(k - cum_f[-1], 0, nt - 1)])
    disp = (cb, ce, cs, cvalid.astype(i32), out_slot)
    blk_tot = jnp.sum(ns, axis=1)
    blk_start = jnp.cumsum(blk_tot) - blk_tot
    ngrp_b = (blk_tot + grp - 1) // grp
    gend = jnp.cumsum(ngrp_b)
    gstart = gend - ngrp_b
    ng = nt // grp + nblk
    g = jnp.arange(ng, dtype=i32)
    bg = jnp.minimum(jnp.sum((gend[None, :] <= g[:, None]).astype(i32), axis=1), nblk - 1)
    gvalid = g < gend[-1]
    lq = ((g - gstart[bg]) * grp)[:, None] + jnp.arange(grp, dtype=i32)[None, :]
    qvalid = gvalid[:, None] & (lq < blk_tot[bg][:, None])
    ci = jnp.clip(blk_start[bg][:, None] + lq, 0, nt - 1)
    slot_q = jnp.where(qvalid, slot[ci], slot[ci[:, :1]])
    comb = (jnp.where(gvalid, bg, nblk - 1), slot_q, jnp.where(qvalid, ce[ci], 0), jnp.where(qvalid, cs[ci], 0),
            qvalid.astype(i32), (gvalid & (g == gstart[bg])).astype(i32))
    return pos_t, gate_t, nt, ng, disp, ffn, comb


def _moe_dispatch_kernel(tb_ref, te_ref, ts_ref, tv_ref, so_ref, h_ref, pos_ref, g_ref, xs_ref, gs_ref, *, sub):
    k = pl.program_id(0)
    e = te_ref[k]
    prow = jnp.where(tv_ref[k] > 0, pos_ref[pl.ds(e, 1), :], -2)
    grow = g_ref[pl.ds(e, 1), :]
    tgt = lax.broadcasted_iota(jnp.int32, (sub, prow.shape[1]), 0) + ts_ref[k] * sub
    hit = prow == tgt
    onehot = jnp.where(hit, 1.0, 0.0).astype(BF16)
    xs_ref[...] = jnp.dot(onehot, h_ref[...], preferred_element_type=F32).astype(xs_ref.dtype)
    gate = jnp.sum(jnp.where(hit, grow, 0.0), axis=1, keepdims=True)
    gs_ref[...] = jnp.broadcast_to(gate, gs_ref.shape)


def _moe_dispatch(h, pos_t, gate_t, nt, disp, tb, sub):
    n_exp = pos_t.shape[1]
    blk = lambda k, b, e, s, v, so: (b[k], 0)
    blk3 = lambda k, b, e, s, v, so: (b[k], 0, 0)
    out = lambda k, b, e, s, v, so: (so[k], 0)
    return pl.pallas_call(
        functools.partial(_moe_dispatch_kernel, sub=sub),
        grid_spec=pltpu.PrefetchScalarGridSpec(
            num_scalar_prefetch=5, grid=(nt,),
            in_specs=[pl.BlockSpec((tb, D), blk),
                      pl.BlockSpec((None, n_exp, tb), blk3),
                      pl.BlockSpec((None, n_exp, tb), blk3)],
            out_specs=[pl.BlockSpec((sub, D), out), pl.BlockSpec((sub, LANES), out)]),
        out_shape=[jax.ShapeDtypeStruct((nt * sub, D), BF16), jax.ShapeDtypeStruct((nt * sub, LANES), F32)],
        compiler_params=_cparams(1),
    )(*disp, h, pos_t, gate_t)


def _moe_up_kernel(te_ref, tv_ref, x_ref, w1_ref, w3_ref, o_ref):
    @pl.when(tv_ref[pl.program_id(1)] > 0)
    def _():
        x = x_ref[...]
        a = jnp.dot(x, w1_ref[...], preferred_element_type=F32)
        b = jnp.dot(x, w3_ref[...], preferred_element_type=F32)
        o_ref[...] = (a * _sigmoid(a) * b).astype(o_ref.dtype)


def _moe_down_kernel(te_ref, tv_ref, a_ref, w2_ref, gs_ref, o_ref):
    @pl.when(tv_ref[pl.program_id(1)] > 0)
    def _():
        y = jnp.dot(a_ref[...], w2_ref[...], preferred_element_type=F32)
        o_ref[...] = (y * gs_ref[:, 0:1]).astype(o_ref.dtype)


def _moe_ffn(xs, gs, w1, w3, w2, e0, ffn, tmf, tf, tn):
    rows = xs.shape[0]
    F = w1.shape[-1]
    ni = rows // tmf
    a = pl.pallas_call(
        _moe_up_kernel,
        grid_spec=pltpu.PrefetchScalarGridSpec(
            num_scalar_prefetch=2, grid=(F // tf, ni),
            in_specs=[pl.BlockSpec((tmf, D), lambda f, i, te, tv: (i, 0)),
                      pl.BlockSpec((None, D, tf), lambda f, i, te, tv: (e0 + te[i], 0, f)),
                      pl.BlockSpec((None, D, tf), lambda f, i, te, tv: (e0 + te[i], 0, f))],
            out_specs=pl.BlockSpec((tmf, tf), lambda f, i, te, tv: (i, f))),
        out_shape=jax.ShapeDtypeStruct((rows, F), BF16),
        compiler_params=_cparams(2),
    )(*ffn, xs, w1, w3)
    return pl.pallas_call(
        _moe_down_kernel,
        grid_spec=pltpu.PrefetchScalarGridSpec(
            num_scalar_prefetch=2, grid=(D // tn, ni),
            in_specs=[pl.BlockSpec((tmf, F), lambda j, i, te, tv: (i, 0)),
                      pl.BlockSpec((None, F, tn), lambda j, i, te, tv: (e0 + te[i], 0, j)),
                      pl.BlockSpec((tmf, LANES), lambda j, i, te, tv: (i, 0))],
            out_specs=pl.BlockSpec((tmf, tn), lambda j, i, te, tv: (i, j))),
        out_shape=jax.ShapeDtypeStruct((rows, D), BF16),
        compiler_params=_cparams(2),
    )(*ffn, a, w2, gs)


def _moe_combine_kernel(cb_ref, sl_ref, ce_ref, cs_ref, cv_ref, cf_ref, *refs, sub, grp):
    ys_refs = refs[:grp]
    pos_ref, x_ref, mg_ref, o_ref = refs[grp:]
    g = pl.program_id(0)

    @pl.when(cf_ref[g] > 0)
    def _():
        o_ref[...] = x_ref[...]

    hots = []
    for q in range(grp):
        prow = jnp.where(cv_ref[g, q] > 0, pos_ref[pl.ds(ce_ref[g, q], 1), :], -2)
        tgt = lax.broadcasted_iota(jnp.int32, (sub, prow.shape[1]), 0) + cs_ref[g, q] * sub
        hots.append(jnp.where(prow == tgt, 1.0, 0.0).astype(BF16))
    onehot = jnp.concatenate(hots, axis=0)
    ys = jnp.concatenate([r[...] for r in ys_refs], axis=0)
    back = lax.dot_general(onehot, ys, (((0,), (0,)), ((), ())), preferred_element_type=F32)
    o_ref[...] += back * mg_ref[...]


def _moe_combine(lay, ys, pos_t, x, mgate, ng, comb, tb, sub, grp):
    M = x.shape[0]
    n_exp = pos_t.shape[1]
    ys_specs = [pl.BlockSpec((sub, D), lambda g, cb, sl, ce, cs, cv, cf, q=q: (sl[g, q], 0)) for q in range(grp)]
    return pl.pallas_call(
        functools.partial(_moe_combine_kernel, sub=sub, grp=grp),
        grid_spec=pltpu.PrefetchScalarGridSpec(
            num_scalar_prefetch=6, grid=(ng,),
            in_specs=ys_specs + [
                pl.BlockSpec((None, n_exp, tb), lambda g, cb, sl, ce, cs, cv, cf: (cb[g], 0, 0)),
                pl.BlockSpec((tb, D), lambda g, cb, sl, ce, cs, cv, cf: (cb[g], 0)),
                pl.BlockSpec((None, 1, D),
                             lambda g, cb, sl, ce, cs, cv, cf: (_group_of_row(lay, cb[g] * tb), 0, 0))],
            out_specs=pl.BlockSpec((tb, D), lambda g, cb, sl, ce, cs, cv, cf: (cb[g], 0))),
        out_shape=jax.ShapeDtypeStruct((M, D), F32),
        compiler_params=_cparams(1),
    )(*comb, *([ys] * grp), pos_t, x, mgate)


def _moe(lay, h, x, mgate, router, w1, w3, w2, e0, n_exp, tm):
    gates = _router(h, router, min(tm, 512))
    tb = _row_tile(lay, MOE_BLOCK)
    sub = min(MOE_SUB, tb)
    tmf = max(min(MOE_FFN_ROWS, tb), sub)
    grp = tmf // sub
    pos_t, gate_t, nt, ng, disp, ffn, comb = _moe_plan(gates, n_exp, tb, sub, grp)
    xs, gs = _moe_dispatch(h, pos_t, gate_t, nt, disp, tb, sub)
    F = w1.shape[-1]
    tf = F // 2 if (F // 2) % LANES == 0 else F
    ys = _moe_ffn(xs, gs, w1, w3, w2, e0, ffn, tmf, tf, 1024)
    return _moe_combine(lay, ys, pos_t, x, mgate, ng, comb, tb, sub, grp)


def kernel(x_prompt, x_sample, state_l0_gla, state_l2_hgrn2, state_l3_rwkv7, c, c_ctx, norm_mix_g, norm_ffn_g, mod_w, mod_b, gla_w_in, gla_gk_w1, gla_gk_w2, gla_gk_b, gla_norm_g, gla_w_out, hy_w_in, hy_conv_w, hy_conv_b, hy_f_w1, hy_f_b1, hy_f_w2, hy_f_b2, hy_f_w3, hy_f_freq, hy_f_bias, hy_w_out, hg_w_in, hg_lb, hg_norm_g, hg_w_out, rw_mu, rw_w_r, rw_w_k, rw_w_v, rw_w_o, rw_w0, rw_w1, rw_w2, rw_a0, rw_a1, rw_a2, rw_g1, rw_g2, rw_k_k, rw_k_a, rw_r_k, rw_ln_w, rw_ln_b, ffn_w1, ffn_w3, ffn_w2, moe_router, moe_w1, moe_w3, moe_w2, final_norm_g):
    lay = Lay(x_prompt.shape[0], x_prompt.shape[1], x_sample.shape[0], x_sample.shape[1])
    M = _M(lay)
    r0 = _R0(lay)
    tm = _row_tile(lay, 1024)
    tm_norm = _row_tile(lay, 512)
    x = jnp.concatenate([x_prompt.reshape(-1, D), x_sample.reshape(-1, D)], axis=0)
    cond = jnp.zeros((8, D), F32).at[0].set(c_ctx).at[1:1 + lay.B1].set(c)
    n_exp = moe_w1.shape[1]
    moe_w1r = moe_w1.reshape((-1,) + moe_w1.shape[2:]).astype(BF16)
    moe_w3r = moe_w3.reshape((-1,) + moe_w3.shape[2:]).astype(BF16)
    moe_w2r = moe_w2.reshape((-1,) + moe_w2.shape[2:]).astype(BF16)
    new_states = {}
    for l in range(DEPTH):
        m = _linear(cond, [mod_w], w_idx=l, tm=8, tn=1024, bias=mod_b[l], pre_silu=True)
        mods = [m[:, k * D:(k + 1) * D].reshape(8, 1, D) for k in range(6)]
        kind = l % 4
        h = _norm(x, norm_mix_g[l], lay=lay, tm=tm_norm, out_dtype=F32 if kind == 3 else BF16,
                  shift=mods[0], scale=mods[1])
        if kind == 0:
            x, new_states[0] = _gla_mixer(lay, h, x, mods[2], state_l0_gla, gla_w_in, gla_gk_w1, gla_gk_w2,
                                          gla_gk_b, gla_norm_g, gla_w_out, tm)
        elif kind == 1:
            x = _hyena_mixer(lay, h, x, mods[2], hy_w_in, hy_conv_w, hy_conv_b, hy_f_w1, hy_f_b1, hy_f_w2,
                             hy_f_b2, hy_f_w3, hy_f_freq, hy_f_bias, hy_w_out, tm)
        elif kind == 2:
            x, new_states[2] = _hgrn2_mixer(lay, h, x, mods[2], state_l2_hgrn2, l, hg_w_in, hg_lb, hg_norm_g,
                                            hg_w_out, tm)
        else:
            x, new_states[3] = _rwkv7_mixer(lay, h, x, mods[2], state_l3_rwkv7, rw_mu, rw_w_r, rw_w_k, rw_w_v,
                                            rw_w_o, rw_w0, rw_w1, rw_w2, rw_a0, rw_a1, rw_a2, rw_g1, rw_g2,
                                            rw_k_k, rw_k_a, rw_r_k, rw_ln_w, rw_ln_b, tm)
        h = _norm(x, norm_ffn_g[l], lay=lay, tm=tm_norm, out_dtype=BF16, shift=mods[3], scale=mods[4])
        j = l // 2
        if l % 2 == 0:
            a = _linear(h, [ffn_w1, ffn_w3], w_idx=j, tm=tm, tn=512, swiglu=True, out_dtype=BF16)
            x = _linear(a, [ffn_w2], w_idx=j, tm=tm, tn=512, res=x, modgate=mods[5], lay=lay)
        else:
            x = _moe(lay, h, x, mods[5], moe_router[j], moe_w1r, moe_w3r, moe_w2r, j * n_exp, n_exp, tm)
    yp = _norm(x, final_norm_g, lay=lay, tm=tm_norm, out_dtype=F32, row0=0, nrows=r0)
    ys = _norm(x, final_norm_g, lay=lay, tm=tm_norm, out_dtype=F32, row0=r0, nrows=M - r0)
    return (yp.reshape(x_prompt.shape), ys.reshape(x_sample.shape), new_states[0], new_states[2],
            new_states[3])
```
